```python
import math
import jax, jax.numpy as jnp
from jax import lax
import numpy as np

D_MODEL = 1024
BATCH = 1
SEQ = 16384
DEPTH = 4

GRID_W = 64
CTX_LEN = 256

SSD_EXPAND = 2
SSD_DI = SSD_EXPAND * D_MODEL
SSD_HEADDIM = 64
SSD_HEADS = SSD_DI // SSD_HEADDIM
SSD_STATE = 128
SSD_GROUPS = 8
SSD_CONV = 3
SSD_CHUNK = 128
SSD_CONV_CH = SSD_DI + 2 * SSD_GROUPS * SSD_STATE
SSD_IN = SSD_DI + SSD_CONV_CH + 2 * SSD_HEADS

NA_HEADS = 16
NA_HEADDIM = 64
NA_DI = NA_HEADS * NA_HEADDIM
WIN_R = 8
WIN_C = 16

N_SSD_LAYERS = (DEPTH + 1) // 2
N_NA_LAYERS = DEPTH // 2
RMS_EPS = 1e-6

kernel_name = "hybrid_ssd_natten_prefix_dit"


def _rms(x, g):
    xf = x.astype(jnp.float32)
    y = xf * lax.rsqrt(jnp.mean(xf * xf, axis=-1, keepdims=True) + RMS_EPS)
    return (y * g.astype(jnp.float32)).astype(x.dtype)


def _dwconv(u, w, b):
    k, ch = w.shape
    out = lax.conv_general_dilated(u, w[:, None, :].astype(u.dtype), window_strides=(1,),
                                   padding=[(k // 2, k // 2)],
                                   dimension_numbers=('NWC', 'WIO', 'NWC'),
                                   feature_group_count=ch)
    return out + b.astype(u.dtype)


def _ssd_scan(x, dt, A, Bm, Cm, h0):
    b, l, H, P = x.shape
    G, N = Bm.shape[-2:]
    K = H // G
    L = SSD_CHUNK
    nc = l // L
    f32 = jnp.float32
    xc = x.astype(f32).reshape(b, nc, L, G, K, P)
    dtc = dt.astype(f32).reshape(b, nc, L, G, K)
    a = dtc * A.astype(f32).reshape(G, K)
    Bc = Bm.astype(f32).reshape(b, nc, L, G, N)
    Cc = Cm.astype(f32).reshape(b, nc, L, G, N)
    a_cum = jnp.cumsum(a, axis=2)
    seg = a_cum[:, :, :, None] - a_cum[:, :, None, :]
    tri = jnp.tril(jnp.ones((L, L), dtype=bool))[None, None, :, :, None, None]
    decay = jnp.exp(jnp.where(tri, seg, -jnp.inf))
    cb = jnp.einsum('bclgn,bcsgn->bclsg', Cc, Bc)
    xdt = xc * dtc[..., None]
    y_diag = jnp.einsum('bclsgk,bcsgkp->bclgkp', cb[..., None] * decay, xdt)
    decay_out = jnp.exp(a_cum[:, :, -1:] - a_cum)
    states = jnp.einsum('bclgn,bclgk,bclgkp->bcgkpn', Bc, decay_out, xdt)
    chunk_decay = jnp.exp(a_cum[:, :, -1])

    def step(h, inp):
        s, d = inp
        return h * d[..., None, None] + s, h

    hT, h_prev = lax.scan(step, h0.astype(f32).reshape(b, G, K, P, N),
                          (jnp.moveaxis(states, 1, 0), jnp.moveaxis(chunk_decay, 1, 0)))
    h_prev = jnp.moveaxis(h_prev, 0, 1)
    y_off = jnp.einsum('bclgn,bcgkpn,bclgk->bclgkp', Cc, h_prev, jnp.exp(a_cum))
    y = (y_diag + y_off).reshape(b, l, H, P)
    return y, hT.reshape(b, H, P, N)


def _ssd_scan_rev(x, dt, A, Bm, Cm, h0):
    y, hT = _ssd_scan(jnp.flip(x, 1), jnp.flip(dt, 1), A, jnp.flip(Bm, 1), jnp.flip(Cm, 1), h0)
    return jnp.flip(y, 1), hT


def _ssd_mixer(h_l, h_c, w_in, conv_w, conv_b, dt_bias, a_log, d_skip, norm_g, w_out, ctx_out):
    GN = SSD_GROUPS * SSD_STATE

    def proj(h):
        b, n, _ = h.shape
        t = h @ w_in
        z = t[..., :SSD_DI]
        xbc = jax.nn.silu(_dwconv(t[..., SSD_DI:SSD_DI + SSD_CONV_CH], conv_w, conv_b))
        xs = xbc[..., :SSD_DI].reshape(b, n, SSD_HEADS, SSD_HEADDIM)
        Bm = xbc[..., SSD_DI:SSD_DI + GN].reshape(b, n, SSD_GROUPS, SSD_STATE)
        Cm = xbc[..., SSD_DI + GN:].reshape(b, n, SSD_GROUPS, SSD_STATE)
        dt = t[..., SSD_DI + SSD_CONV_CH:].astype(jnp.float32).reshape(b, n, 2, SSD_HEADS)
        dt = jax.nn.softplus(dt + dt_bias.astype(jnp.float32))
        return z, xs, Bm, Cm, dt

    def finish(y, xs, z):
        b, n = xs.shape[:2]
        y = y + xs.astype(jnp.float32) * d_skip.astype(jnp.float32)[:, None]
        y = (y.reshape(b, n, SSD_DI) * jax.nn.silu(z.astype(jnp.float32)))
        y = y.reshape(b, n, SSD_GROUPS, SSD_DI // SSD_GROUPS)
        y = y * lax.rsqrt(jnp.mean(y * y, axis=-1, keepdims=True) + RMS_EPS)
        y = (y.reshape(b, n, SSD_DI) * norm_g.astype(jnp.float32)).astype(xs.dtype)
        return y @ w_out

    A = -jnp.exp(a_log.astype(jnp.float32))
    zc, xc, Bc, Cc, dtc = proj(h_c)
    zl, xl, Bl, Cl, dtl = proj(h_l)
    h0 = jnp.zeros((h_l.shape[0], SSD_HEADS, SSD_HEADDIM, SSD_STATE), jnp.float32)
    yc_f, s_f = _ssd_scan(xc, dtc[:, :, 0], A[0], Bc, Cc, h0)
    yc_b, s_b = _ssd_scan_rev(xc, dtc[:, :, 1], A[1], Bc, Cc, h0)
    yl_f, _ = _ssd_scan(xl, dtl[:, :, 0], A[0], Bl, Cl, s_f)
    yl_b, _ = _ssd_scan_rev(xl, dtl[:, :, 1], A[1], Bl, Cl, s_b)
    out_l = finish(yl_f + yl_b, xl, zl)
    out_c = finish(yc_f + yc_b, xc, zc) if ctx_out else None
    return out_l, out_c


def _na_mixer(h_l, h_c, w_in, rpb, w_out, ctx_out):
    b, n, _ = h_l.shape
    rows = n // GRID_W
    kr = min(WIN_R, rows)
    scale = NA_HEADDIM ** -0.5
    tl = (h_l @ w_in).reshape(b, n, 4, NA_HEADS, NA_HEADDIM)
    tc = (h_c @ w_in).reshape(b, h_c.shape[1], 4, NA_HEADS, NA_HEADDIM)
    q_l, k_l, v_l, g_l = tl[:, :, 0], tl[:, :, 1], tl[:, :, 2], tl[:, :, 3]
    q_c, k_c, v_c, g_c = tc[:, :, 0], tc[:, :, 1], tc[:, :, 2], tc[:, :, 3]

    grid = (b, rows, GRID_W, NA_HEADS, NA_HEADDIM)
    qg, kgrid, vgrid = q_l.reshape(grid), k_l.reshape(grid), v_l.reshape(grid)
    r_idx = jnp.arange(rows)
    row_start = jnp.clip(r_idx - kr // 2, 0, rows - kr)
    row_sel = row_start[:, None] + jnp.arange(kr)[None, :]
    kg = kgrid[:, row_sel]
    vg = vgrid[:, row_sel]

    cols = jnp.arange(GRID_W)
    col_start = jnp.clip(cols - WIN_C // 2, 0, GRID_W - WIN_C)
    col_mask = (cols[None, :] >= col_start[:, None]) & (cols[None, :] < col_start[:, None] + WIN_C)
    roff = row_sel - r_idx[:, None] + (WIN_R - 1)
    coff = jnp.clip(cols[None, :] - cols[:, None], -(WIN_C - 1), WIN_C - 1) + (WIN_C - 1)
    bias = rpb.astype(jnp.float32)[:, roff[:, None, :, None], coff[None, :, None, :]]

    s_win = jnp.einsum('brqhd,brkwhd->bhrqkw', qg, kg).astype(jnp.float32) * scale + bias[None]
    s_win = jnp.where(col_mask[None, None, None, :, None, :], s_win, -jnp.inf)
    s_ctx = jnp.einsum('brqhd,bchd->bhrqc', qg, k_c).astype(jnp.float32) * scale
    s = jnp.concatenate([s_win.reshape(b, NA_HEADS, rows, GRID_W, kr * GRID_W), s_ctx], axis=-1)
    p = jax.nn.softmax(s, axis=-1).astype(v_l.dtype)
    p_win = p[..., :kr * GRID_W].reshape(b, NA_HEADS, rows, GRID_W, kr, GRID_W)
    p_ctx = p[..., kr * GRID_W:]
    o = jnp.einsum('bhrqkw,brkwhd->brqhd', p_win, vg) + jnp.einsum('bhrqc,bchd->brqhd', p_ctx, v_c)
    out_l = (o.reshape(b, n, NA_DI) * jax.nn.silu(g_l.reshape(b, n, NA_DI))) @ w_out

    out_c = None
    if ctx_out:
        sc = jnp.einsum('bqhd,bkhd->bhqk', q_c, k_c).astype(jnp.float32) * scale
        pc = jax.nn.softmax(sc, axis=-1).astype(v_c.dtype)
        oc = jnp.einsum('bhqk,bkhd->bqhd', pc, v_c).reshape(b, -1, NA_DI)
        out_c = (oc * jax.nn.silu(g_c.reshape(b, -1, NA_DI))) @ w_out
    return out_l, out_c


def setup_inputs(seed: int = 0) -> dict:
    key = jax.random.key(seed)
    ks = jax.random.split(key, 24)
    D = D_MODEL
    nrm = jax.random.normal
    x = nrm(ks[0], (BATCH, SEQ, D), jnp.float32)
    c = nrm(ks[1], (BATCH, D), jnp.float32)
    ctx = nrm(ks[2], (BATCH, CTX_LEN, D), jnp.float32)
    c_ctx = nrm(ks[3], (D,), jnp.float32)
    ada_w = nrm(ks[4], (DEPTH, D, 3 * D), jnp.float32) * D ** -0.5
    ada_b = nrm(ks[5], (DEPTH, 3 * D), jnp.float32) * 0.02
    pre_g = 1.0 + 0.02 * nrm(ks[6], (DEPTH, D), jnp.float32)
    post_g = 1.0 + 0.02 * nrm(ks[7], (DEPTH, D), jnp.float32)
    ssd_w_in = nrm(ks[8], (N_SSD_LAYERS, D, SSD_IN), jnp.float32) * D ** -0.5
    ssd_conv_w = nrm(ks[9], (N_SSD_LAYERS, SSD_CONV, SSD_CONV_CH), jnp.float32) * SSD_CONV ** -0.5
    ssd_conv_b = nrm(ks[10], (N_SSD_LAYERS, SSD_CONV_CH), jnp.float32) * 0.02
    u = jax.random.uniform(ks[11], (N_SSD_LAYERS, 2, SSD_HEADS), jnp.float32)
    dt0 = jnp.exp(u * (math.log(0.1) - math.log(0.001)) + math.log(0.001))
    ssd_dt_bias = dt0 + jnp.log(-jnp.expm1(-dt0))
    ssd_a_log = jnp.log(jax.random.uniform(ks[12], (N_SSD_LAYERS, 2, SSD_HEADS), jnp.float32, 1.0, 16.0))
    ssd_d = 1.0 + 0.1 * nrm(ks[13], (N_SSD_LAYERS, SSD_HEADS), jnp.float32)
    ssd_norm_g = 1.0 + 0.02 * nrm(ks[14], (N_SSD_LAYERS, SSD_DI), jnp.float32)
    ssd_w_out = nrm(ks[15], (N_SSD_LAYERS, SSD_DI, D), jnp.float32) * SSD_DI ** -0.5
    na_w_in = nrm(ks[16], (N_NA_LAYERS, D, 4 * NA_DI), jnp.float32) * D ** -0.5
    na_rpb = nrm(ks[17], (N_NA_LAYERS, NA_HEADS, 2 * WIN_R - 1, 2 * WIN_C - 1), jnp.float32) * 0.1
    na_w_out = nrm(ks[18], (N_NA_LAYERS, NA_DI, D), jnp.float32) * NA_DI ** -0.5
    return {"x": x, "c": c, "ctx": ctx, "c_ctx": c_ctx,
            "ada_w": ada_w, "ada_b": ada_b, "pre_g": pre_g, "post_g": post_g,
            "ssd_w_in": ssd_w_in, "ssd_conv_w": ssd_conv_w, "ssd_conv_b": ssd_conv_b,
            "ssd_dt_bias": ssd_dt_bias, "ssd_a_log": ssd_a_log, "ssd_d": ssd_d,
            "ssd_norm_g": ssd_norm_g, "ssd_w_out": ssd_w_out,
            "na_w_in": na_w_in, "na_rpb": na_rpb, "na_w_out": na_w_out}


def reference(x, c, ctx, c_ctx, ada_w, ada_b, pre_g, post_g,
              ssd_w_in, ssd_conv_w, ssd_conv_b, ssd_dt_bias, ssd_a_log, ssd_d,
              ssd_norm_g, ssd_w_out, na_w_in, na_rpb, na_w_out):
    D = D_MODEL
    sc_l = jax.nn.silu(c)
    sc_c = jax.nn.silu(c_ctx)
    for i in range(DEPTH):
        last = i == DEPTH - 1
        mod_l = sc_l @ ada_w[i] + ada_b[i]
        mod_c = sc_c @ ada_w[i] + ada_b[i]
        shift_l, scale_l, gate_l = (mod_l[:, None, :D], mod_l[:, None, D:2 * D], mod_l[:, None, 2 * D:])
        shift_c, scale_c, gate_c = mod_c[:D], mod_c[D:2 * D], mod_c[2 * D:]
        h_l = _rms(x, pre_g[i]) * (1.0 + scale_l) + shift_l
        h_c = _rms(ctx, pre_g[i]) * (1.0 + scale_c) + shift_c
        j = i // 2
        if i % 2 == 0:
            out_l, out_c = _ssd_mixer(h_l, h_c, ssd_w_in[j], ssd_conv_w[j], ssd_conv_b[j],
                                      ssd_dt_bias[j], ssd_a_log[j], ssd_d[j], ssd_norm_g[j],
                                      ssd_w_out[j], not last)
        else:
            out_l, out_c = _na_mixer(h_l, h_c, na_w_in[j], na_rpb[j], na_w_out[j], not last)
        x = x + gate_l * _rms(out_l, post_g[i])
        if not last:
            ctx = ctx + gate_c * _rms(out_c, post_g[i])
    return x
```

```python
import functools
import math

import jax
import jax.numpy as jnp
from jax import lax
from jax.experimental import pallas as pl
from jax.experimental.pallas import tpu as pltpu

D_MODEL = 1024
SEQ = 16384
DEPTH = 4
GRID_W = 64
GRID_H = SEQ // GRID_W
CTX_LEN = 256
ROWS = CTX_LEN + SEQ

SSD_DI = 2048
SSD_HEADDIM = 64
SSD_HEADS = 32
SSD_STATE = 128
SSD_GROUPS = 8
SSD_HEADS_PER_GROUP = SSD_HEADS // SSD_GROUPS
SSD_GROUP_W = SSD_DI // SSD_GROUPS
SSD_GN = SSD_GROUPS * SSD_STATE
SSD_CONV_CH = SSD_DI + 2 * SSD_GN
SSD_IN = SSD_DI + SSD_CONV_CH + 2 * SSD_HEADS
SSD_DT_PAD = 128

NA_HEADS = 16
NA_HEADDIM = 64
NA_DI = 1024
WIN_R = 8
WIN_C = 16

RMS_EPS = 1e-6
NEG_BIG = -1e30

LANES = 128
ROW_TILE = 256
SSD_CHUNK = 128
NA_QROWS = 4
NA_QBLK = NA_QROWS * GRID_W
VMEM_LIMIT = 56 * 1024 * 1024

F32 = jnp.float32
BF16 = jnp.bfloat16


def _silu(x):
    return x / (1.0 + jnp.exp(-x))


def _softplus(x):
    return jnp.maximum(x, 0.0) + jnp.log(1.0 + jnp.exp(-jnp.abs(x)))


def _dot(a, b):
    return jnp.dot(a, b, preferred_element_type=F32)


def _dot_nt(a, b):
    return lax.dot_general(a, b, (((1,), (1,)), ((), ())), preferred_element_type=F32)


def _split3(v):
    hi = v.astype(BF16)
    r1 = v - hi.astype(F32)
    mid = r1.astype(BF16)
    lo = (r1 - mid.astype(F32)).astype(BF16)
    return hi, mid, lo


def _dot_exact_rhs(sel, v):
    hi, mid, lo = _split3(v)
    return _dot(sel, hi) + _dot(sel, mid) + _dot(sel, lo)


def _dot_exact_lhs(v, sel):
    hi, mid, lo = _split3(v)
    return _dot(hi, sel) + _dot(mid, sel) + _dot(lo, sel)


def _mod_kernel(c_ref, w_ref, b_ref, o_ref):
    s = _silu(c_ref[...]).astype(BF16)
    o_ref[0] = _dot(s, w_ref[0].astype(BF16)) + b_ref[0]


def _modulation(cvec, ada_w, ada_b):
    tn = 1024
    return pl.pallas_call(
        _mod_kernel,
        name="modulation",
        grid=(DEPTH, 3 * D_MODEL // tn),
        in_specs=[
            pl.BlockSpec((8, D_MODEL), lambda i, j: (0, 0)),
            pl.BlockSpec((1, D_MODEL, tn), lambda i, j: (i, 0, j)),
            pl.BlockSpec((1, 1, tn), lambda i, j: (i, 0, j)),
        ],
        out_specs=pl.BlockSpec((1, 8, tn), lambda i, j: (i, 0, j)),
        out_shape=jax.ShapeDtypeStruct((DEPTH, 8, 3 * D_MODEL), F32),
        compiler_params=pltpu.CompilerParams(vmem_limit_bytes=VMEM_LIMIT),
    )(cvec, ada_w, ada_b.reshape(DEPTH, 1, 3 * D_MODEL))


def _mod_rows(mod_ref, is_ctx):
    m = mod_ref[0]
    row = jnp.where(is_ctx, m[1:2, :], m[0:1, :])
    return row[:, :D_MODEL], row[:, D_MODEL:2 * D_MODEL], row[:, 2 * D_MODEL:]


def _prenorm(x_ref, g_ref, mod_ref):
    x = x_ref[...]
    is_ctx = pl.program_id(0) * ROW_TILE < CTX_LEN
    shift, scale, _ = _mod_rows(mod_ref, is_ctx)
    y = x * lax.rsqrt(jnp.mean(x * x, axis=-1, keepdims=True) + RMS_EPS) * g_ref[...]
    return (y * (1.0 + scale) + shift).astype(BF16)


def _postnorm_residual(o, r_ref, g_ref, mod_ref, out_ref):
    is_ctx = pl.program_id(0) * ROW_TILE < CTX_LEN
    _, _, gate = _mod_rows(mod_ref, is_ctx)
    y = o * lax.rsqrt(jnp.mean(o * o, axis=-1, keepdims=True) + RMS_EPS) * g_ref[...]
    out_ref[...] = r_ref[...] + gate * y


def _row_spec(width):
    return pl.BlockSpec((ROW_TILE, width), lambda i: (i, 0))


def _const_spec(shape):
    nd = len(shape)
    return pl.BlockSpec(shape, lambda i: (0,) * nd)


def _layer_spec(shape, layer):
    nd = len(shape)
    return pl.BlockSpec((1,) + shape, lambda i: (layer,) + (0,) * nd)


_COL_CHUNK = 1024


def _ssd_in_kernel(x_ref, g_ref, mod_ref, w_ref, wdt_ref, z_ref, xbc_ref, dt_ref, dtt_ref):
    h = _prenorm(x_ref, g_ref, mod_ref)
    for c in range(0, SSD_DI, _COL_CHUNK):
        z_ref[:, c:c + _COL_CHUNK] = _dot(h, w_ref[:, c:c + _COL_CHUNK])
    for c in range(0, SSD_CONV_CH, _COL_CHUNK):
        xbc_ref[:, c:c + _COL_CHUNK] = _dot(h, w_ref[:, SSD_DI + c:SSD_DI + c + _COL_CHUNK])
    dt_ref[...] = _dot(h, w_ref[:, SSD_DI + SSD_CONV_CH:])
    dtt_ref[...] = _dot_nt(wdt_ref[...], h)


def _ssd_in(r, pre_g, mod, layer, w_pad, wdt_t):
    n_in = SSD_DI + SSD_CONV_CH + SSD_DT_PAD
    return pl.pallas_call(
        _ssd_in_kernel,
        name="ssd_in",
        grid=(ROWS // ROW_TILE,),
        in_specs=[
            _row_spec(D_MODEL),
            _const_spec((1, D_MODEL)),
            _layer_spec((8, 3 * D_MODEL), layer),
            pl.BlockSpec((D_MODEL, n_in), lambda i: (0, 0), pipeline_mode=pl.Buffered(1)),
            _const_spec((SSD_DT_PAD, D_MODEL)),
        ],
        out_specs=[
            _row_spec(SSD_DI),
            _row_spec(SSD_CONV_CH),
            _row_spec(SSD_DT_PAD),
            pl.BlockSpec((SSD_DT_PAD, ROW_TILE), lambda i: (0, i)),
        ],
        out_shape=[
            jax.ShapeDtypeStruct((ROWS, SSD_DI), F32),
            jax.ShapeDtypeStruct((ROWS, SSD_CONV_CH), F32),
            jax.ShapeDtypeStruct((ROWS, SSD_DT_PAD), F32),
            jax.ShapeDtypeStruct((SSD_DT_PAD, ROWS), F32),
        ],
        compiler_params=pltpu.CompilerParams(vmem_limit_bytes=VMEM_LIMIT),
    )(r, pre_g, mod, w_pad, wdt_t)


_N_CHUNKS = ROWS // SSD_CHUNK
_N_CTX_CHUNKS = CTX_LEN // SSD_CHUNK
_HALO = 8


def _fwd_chunk(i):
    return i


def _bwd_chunk(i):
    return jnp.where(i < _N_CTX_CHUNKS, _N_CTX_CHUNKS - 1 - i, _N_CHUNKS - 1 + _N_CTX_CHUNKS - i)


def _conv_silu(raw, prev8, next8, chunk, cw_ref, cb_ref):
    t = SSD_CHUNK
    first = (chunk == 0) | (chunk == _N_CTX_CHUNKS)
    last = (chunk == _N_CTX_CHUNKS - 1) | (chunk == _N_CHUNKS - 1)
    prev_row = jnp.where(first, 0.0, prev8[_HALO - 1:_HALO, :])
    next_row = jnp.where(last, 0.0, next8[0:1, :])
    rid = lax.broadcasted_iota(jnp.int32, (t, 1), 0)
    up = jnp.where(rid == 0, prev_row, pltpu.roll(raw, 1, axis=0))
    dn = jnp.where(rid == t - 1, next_row, pltpu.roll(raw, t - 1, axis=0))
    acc = cw_ref[0:1, :] * up + cw_ref[1:2, :] * raw + cw_ref[2:3, :] * dn + cb_ref[...]
    return _silu(acc)


def _ssd_direction(d, chunk, raw_ref, prev_ref, next_ref, dt_ref, dtt_ref,
                   cw_ref, cb_ref, dtb_row_ref, dtb_col_ref, alog_row_ref, alog_col_ref,
                   dskip_ref, exp_ref, state_ref, y_ref):
    t = SSD_CHUNK
    act = _conv_silu(raw_ref[...], prev_ref[...], next_ref[...], chunk, cw_ref, cb_ref)
    xs = act[:, :SSD_DI]
    xs_b = xs.astype(BF16)
    bm = act[:, SSD_DI:SSD_DI + SSD_GN]
    cm_b = act[:, SSD_DI + SSD_GN:].astype(BF16)

    dt_tok = _softplus(dt_ref[...] + dtb_row_ref[...])
    dt_head = _softplus(dtt_ref[...] + dtb_col_ref[...])
    a_tok = dt_tok * (-jnp.exp(alog_row_ref[...]))
    a_head = dt_head * (-jnp.exp(alog_col_ref[...]))

    li = lax.broadcasted_iota(jnp.int32, (t, t), 0)
    si = lax.broadcasted_iota(jnp.int32, (t, t), 1)
    lower = si <= li
    upper = si >= li
    mask = lower if d == 0 else upper
    tri = jnp.where(mask, 1.0, 0.0).astype(BF16)
    tri_t = jnp.where(upper if d == 0 else lower, 1.0, 0.0).astype(BF16)

    cum = _dot_exact_rhs(tri, a_tok)
    cum_t = _dot_exact_lhs(a_head, tri_t)
    tot = cum[t - 1:t, :] if d == 0 else cum[0:1, :]

    expand = exp_ref[d]
    e_in = _dot_exact_lhs(jnp.exp(cum), expand)
    w_out = _dot_exact_lhs(jnp.exp(tot - cum) * dt_tok, expand)
    e_tot = _dot_exact_lhs(jnp.broadcast_to(jnp.exp(tot), (8, LANES)), expand)[0:1, :]

    xw_b = (xs * w_out).astype(BF16)
    lane = lax.broadcasted_iota(jnp.int32, (1, LANES), 1)
    first_head = lane < SSD_HEADDIM

    for g in range(SSD_GROUPS):
        gs = slice(g * SSD_GROUP_W, (g + 1) * SSD_GROUP_W)
        ns = slice(g * SSD_STATE, (g + 1) * SSD_STATE)
        b_g = bm[:, ns]
        c_g = cm_b[:, ns]
        state = state_ref[d, g]
        y_off = _dot(c_g, state.astype(BF16)) * e_in[:, gs]
        state_ref[d, g] = state * e_tot[:, gs] + _dot(b_g.T.astype(BF16), xw_b[:, gs])
        cb = _dot_nt(c_g, b_g.astype(BF16))
        y_pairs = []
        for pair in range(SSD_HEADS_PER_GROUP // 2):
            ys = []
            for k in range(2):
                idx = d * SSD_HEADS + g * SSD_HEADS_PER_GROUP + 2 * pair + k
                seg = cum[:, idx:idx + 1] - cum_t[idx:idx + 1, :]
                decay = jnp.exp(jnp.where(mask, seg, NEG_BIG))
                m = (cb * decay * dt_head[idx:idx + 1, :]).astype(BF16)
                c0 = g * SSD_GROUP_W + pair * LANES
                ys.append(_dot(m, xs_b[:, c0:c0 + LANES]))
            y_pairs.append(jnp.where(first_head, ys[0], ys[1]))
        y = jnp.concatenate(y_pairs, axis=1) + y_off
        if d == 0:
            y = y + xs[:, gs] * dskip_ref[:, gs]
        y_ref[:, gs] = y


def _ssd_scan_kernel(raw_f, prev_f, next_f, dt_f, dtt_f, raw_b, prev_b, next_b, dt_b, dtt_b,
                     cw_ref, cb_ref, dtb_row_ref, dtb_col_ref, alog_row_ref, alog_col_ref,
                     dskip_ref, exp_ref, yf_ref, yb_ref, state_ref):
    i = pl.program_id(0)

    @pl.when(i == 0)
    def _():
        state_ref[...] = jnp.zeros_like(state_ref)

    params = (cw_ref, cb_ref, dtb_row_ref, dtb_col_ref, alog_row_ref, alog_col_ref,
              dskip_ref, exp_ref, state_ref)
    _ssd_direction(0, _fwd_chunk(i), raw_f, prev_f, next_f, dt_f, dtt_f, *params, yf_ref)
    _ssd_direction(1, _bwd_chunk(i), raw_b, prev_b, next_b, dt_b, dtt_b, *params, yb_ref)


def _ssd_scan(xbc, dt, dtt, conv_w, conv_b, dtb_row, dtb_col, alog_row, alog_col, dskip, expand):
    t = SSD_CHUNK
    per8 = t // _HALO
    last8 = ROWS // _HALO - 1

    def chunk_specs(order):
        return [
            pl.BlockSpec((t, SSD_CONV_CH), lambda i: (order(i), 0)),
            pl.BlockSpec((_HALO, SSD_CONV_CH), lambda i: (jnp.maximum(order(i) * per8 - 1, 0), 0)),
            pl.BlockSpec((_HALO, SSD_CONV_CH),
                         lambda i: (jnp.minimum((order(i) + 1) * per8, last8), 0)),
            pl.BlockSpec((t, SSD_DT_PAD), lambda i: (order(i), 0)),
            pl.BlockSpec((SSD_DT_PAD, t), lambda i: (0, order(i))),
        ]

    y_shape = jax.ShapeDtypeStruct((ROWS, SSD_DI), F32)
    return pl.pallas_call(
        _ssd_scan_kernel,
        name="ssd_scan",
        grid=(_N_CHUNKS,),
        in_specs=chunk_specs(_fwd_chunk) + chunk_specs(_bwd_chunk) + [
            _const_spec((3, SSD_CONV_CH)),
            _const_spec((1, SSD_CONV_CH)),
            _const_spec((1, SSD_DT_PAD)),
            _const_spec((SSD_DT_PAD, 1)),
            _const_spec((1, SSD_DT_PAD)),
            _const_spec((SSD_DT_PAD, 1)),
            _const_spec((1, SSD_DI)),
            _const_spec((2, SSD_DT_PAD, SSD_DI)),
        ],
        out_specs=[
            pl.BlockSpec((t, SSD_DI), lambda i: (_fwd_chunk(i), 0)),
            pl.BlockSpec((t, SSD_DI), lambda i: (_bwd_chunk(i), 0)),
        ],
        out_shape=[y_shape, y_shape],
        scratch_shapes=[pltpu.VMEM((2, SSD_GROUPS, SSD_STATE, SSD_GROUP_W), F32)],
        compiler_params=pltpu.CompilerParams(
            dimension_semantics=("arbitrary",), vmem_limit_bytes=VMEM_LIMIT),
    )(xbc, xbc, xbc, dt, dtt, xbc, xbc, xbc, dt, dtt,
      conv_w, conv_b, dtb_row, dtb_col, alog_row, alog_col, dskip, expand)


def _ssd_out_kernel(yf_ref, yb_ref, z_ref, ng_ref, w_ref, r_ref, pg_ref, mod_ref, out_ref):
    y = (yf_ref[...] + yb_ref[...]) * _silu(z_ref[...])
    parts = []
    for g in range(SSD_GROUPS):
        yg = y[:, g * SSD_GROUP_W:(g + 1) * SSD_GROUP_W]
        parts.append(yg * lax.rsqrt(jnp.mean(yg * yg, axis=-1, keepdims=True) + RMS_EPS))
    yn = (jnp.concatenate(parts, axis=1) * ng_ref[...]).astype(BF16)
    _postnorm_residual(_dot(yn, w_ref[...]), r_ref, pg_ref, mod_ref, out_ref)


def _ssd_out(yf, yb, z, norm_g, w_out, r, post_g, mod, layer):
    return pl.pallas_call(
        _ssd_out_kernel,
        name="ssd_out",
        grid=(ROWS // ROW_TILE,),
        in_specs=[
            _row_spec(SSD_DI), _row_spec(SSD_DI), _row_spec(SSD_DI),
            _const_spec((1, SSD_DI)),
            _const_spec((SSD_DI, D_MODEL)),
            _row_spec(D_MODEL),
            _const_spec((1, D_MODEL)),
            _layer_spec((8, 3 * D_MODEL), layer),
        ],
        out_specs=_row_spec(D_MODEL),
        out_shape=jax.ShapeDtypeStruct((ROWS, D_MODEL), F32),
        compiler_params=pltpu.CompilerParams(vmem_limit_bytes=VMEM_LIMIT),
    )(yf, yb, z, norm_g, w_out, r, post_g, mod)


def _na_in_kernel(x_ref, g_ref, mod_ref, w_ref, qkv_ref, gate_ref):
    h = _prenorm(x_ref, g_ref, mod_ref)
    scale = NA_HEADDIM ** -0.5
    qkv_ref[:, :NA_DI] = (_dot(h, w_ref[:, :NA_DI]) * scale).astype(BF16)
    for c in range(NA_DI, 3 * NA_DI, _COL_CHUNK):
        qkv_ref[:, c:c + _COL_CHUNK] = _dot(h, w_ref[:, c:c + _COL_CHUNK]).astype(BF16)
    gate_ref[...] = _dot(h, w_ref[:, 3 * NA_DI:])


def _na_in(r, pre_g, mod, layer, w_in):
    return pl.pallas_call(
        _na_in_kernel,
        name="na_in",
        grid=(ROWS // ROW_TILE,),
        in_specs=[
            _row_spec(D_MODEL),
            _const_spec((1, D_MODEL)),
            _layer_spec((8, 3 * D_MODEL), layer),
            pl.BlockSpec((D_MODEL, 4 * NA_DI), lambda i: (0, 0), pipeline_mode=pl.Buffered(1)),
        ],
        out_specs=[_row_spec(3 * NA_DI), _row_spec(NA_DI)],
        out_shape=[
            jax.ShapeDtypeStruct((ROWS, 3 * NA_DI), BF16),
            jax.ShapeDtypeStruct((ROWS, NA_DI), F32),
        ],
        compiler_params=pltpu.CompilerParams(vmem_limit_bytes=VMEM_LIMIT),
    )(r, pre_g, mod, w_in)


_NA_BLOCKS = ROWS // NA_QBLK
_NA_WIN_KEYS = 3 * NA_QBLK
_N_HEAD_PAIRS = NA_HEADS // 2
_VAR_TOP, _VAR_MID, _VAR_BOTTOM, _VAR_CTX = 0, 1, 2, 3


def _na_bias_table(rpb):
    cols = jnp.arange(GRID_W)
    col_start = jnp.clip(cols - WIN_C // 2, 0, GRID_W - WIN_C)
    cvalid = (cols[None, :] >= col_start[:, None]) & (cols[None, :] < col_start[:, None] + WIN_C)
    coff = jnp.clip(cols[None, :] - cols[:, None], -(WIN_C - 1), WIN_C - 1) + (WIN_C - 1)
    rpb = rpb.astype(F32)

    def variant(r0):
        qr = r0 + jnp.arange(NA_QROWS)
        kr = r0 - NA_QROWS + jnp.arange(3 * NA_QROWS)
        start = jnp.clip(qr - WIN_R // 2, 0, GRID_H - WIN_R)
        rvalid = (kr[None, :] >= start[:, None]) & (kr[None, :] < start[:, None] + WIN_R)
        roff = jnp.clip(kr[None, :] - qr[:, None] + (WIN_R - 1), 0, 2 * WIN_R - 2)
        b = rpb[:, roff[:, None, :, None], coff[None, :, None, :]]
        valid = rvalid[:, None, :, None] & cvalid[None, :, None, :]
        b = jnp.where(valid[None], b, NEG_BIG)
        return b.reshape(NA_HEADS, NA_QBLK, _NA_WIN_KEYS)

    ctx_variant = jnp.full((NA_HEADS, NA_QBLK, _NA_WIN_KEYS), NEG_BIG, F32)
    return jnp.stack([variant(0), variant(2 * NA_QROWS), variant(GRID_H - NA_QROWS), ctx_variant])


def _na_kernel(q_ref, kp_ref, kc_ref, kn_ref, kx_ref, vp_ref, vc_ref, vn_ref, vx_ref,
               gate_ref, bias_ref, o_ref):
    q = q_ref[...]
    lane = lax.broadcasted_iota(jnp.int32, (1, LANES), 1)
    first_head = lane < NA_HEADDIM
    keys = (kp_ref[...], kc_ref[...], kn_ref[...], kx_ref[...])
    vals = (vp_ref[...], vc_ref[...], vn_ref[...], vx_ref[...])
    outs = []
    for e in range(2):
        qe = jnp.where(first_head if e == 0 else ~first_head, q, jnp.zeros_like(q))
        bias = bias_ref[0, e]
        s = [_dot_nt(qe, k) for k in keys]
        for j in range(3):
            s[j] = s[j] + bias[:, j * NA_QBLK:(j + 1) * NA_QBLK]
        m = functools.reduce(jnp.maximum, [jnp.max(x, axis=-1, keepdims=True) for x in s])
        p = [jnp.exp(x - m) for x in s]
        denom = functools.reduce(jnp.add, [jnp.sum(x, axis=-1, keepdims=True) for x in p])
        inv = 1.0 / denom
        o = functools.reduce(jnp.add, [_dot((x * inv).astype(BF16), v) for x, v in zip(p, vals)])
        outs.append(o)
    o = jnp.where(first_head, outs[0], outs[1])
    o_ref[...] = (o * _silu(gate_ref[...])).astype(BF16)


def _na_attention(qkv, gate, bias):
    nq = NA_DI // LANES

    def variant(rb):
        return jnp.where(rb == 0, _VAR_CTX,
                         jnp.where(rb == 1, _VAR_TOP,
                                   jnp.where(rb == _NA_BLOCKS - 1, _VAR_BOTTOM, _VAR_MID)))

    def blk(row_fn, section):
        return pl.BlockSpec((NA_QBLK, LANES), lambda hp, rb: (row_fn(rb), section * nq + hp))

    prev_rb = lambda rb: jnp.maximum(rb - 1, 0)
    next_rb = lambda rb: jnp.minimum(rb + 1, _NA_BLOCKS - 1)
    same_rb = lambda rb: rb
    ctx_rb = lambda rb: 0
    return pl.pallas_call(
        _na_kernel,
        name="na_attention",
        grid=(_N_HEAD_PAIRS, _NA_BLOCKS),
        in_specs=[
            blk(same_rb, 0),
            blk(prev_rb, 1), blk(same_rb, 1), blk(next_rb, 1), blk(ctx_rb, 1),
            blk(prev_rb, 2), blk(same_rb, 2), blk(next_rb, 2), blk(ctx_rb, 2),
            pl.BlockSpec((NA_QBLK, LANES), lambda hp, rb: (rb, hp)),
            pl.BlockSpec((1, 2, NA_QBLK, _NA_WIN_KEYS), lambda hp, rb: (variant(rb), hp, 0, 0)),
        ],
        out_specs=pl.BlockSpec((NA_QBLK, LANES), lambda hp, rb: (rb, hp)),
        out_shape=jax.ShapeDtypeStruct((ROWS, NA_DI), BF16),
        compiler_params=pltpu.CompilerParams(vmem_limit_bytes=VMEM_LIMIT),
    )(qkv, qkv, qkv, qkv, qkv, qkv, qkv, qkv, qkv, gate, bias)


def _na_out_kernel(a_ref, w_ref, r_ref, pg_ref, mod_ref, out_ref):
    _postnorm_residual(_dot(a_ref[...], w_ref[...]), r_ref, pg_ref, mod_ref, out_ref)


def _na_out(a, w_out, r, post_g, mod, layer):
    return pl.pallas_call(
        _na_out_kernel,
        name="na_out",
        grid=(ROWS // ROW_TILE,),
        in_specs=[
            _row_spec(NA_DI),
            _const_spec((NA_DI, D_MODEL)),
            _row_spec(D_MODEL),
            _const_spec((1, D_MODEL)),
            _layer_spec((8, 3 * D_MODEL), layer),
        ],
        out_specs=_row_spec(D_MODEL),
        out_shape=jax.ShapeDtypeStruct((ROWS, D_MODEL), F32),
        compiler_params=pltpu.CompilerParams(vmem_limit_bytes=VMEM_LIMIT),
    )(a, w_out, r, post_g, mod)


def _pad_lanes_row(v):
    flat = v.astype(F32).reshape(1, 2 * SSD_HEADS)
    return jnp.pad(flat, ((0, 0), (0, SSD_DT_PAD - 2 * SSD_HEADS)))


def _head_expand():
    rows = jnp.arange(SSD_DT_PAD)[:, None]
    head_of_col = jnp.arange(SSD_DI)[None, :] // SSD_HEADDIM
    return jnp.stack([(rows == head_of_col + d * SSD_HEADS) for d in range(2)]).astype(BF16)


def kernel(x, c, ctx, c_ctx, ada_w, ada_b, pre_g, post_g, ssd_w_in, ssd_conv_w, ssd_conv_b,
           ssd_dt_bias, ssd_a_log, ssd_d, ssd_norm_g, ssd_w_out, na_w_in, na_rpb, na_w_out):
    assert x.shape == (1, SEQ, D_MODEL) and ctx.shape == (1, CTX_LEN, D_MODEL)
    r = jnp.concatenate([ctx[0], x[0]], axis=0)
    cvec = jnp.zeros((8, D_MODEL), F32).at[0].set(c[0]).at[1].set(c_ctx)
    mod = _modulation(cvec, ada_w, ada_b)
    expand = _head_expand()

    for i in range(DEPTH):
        j = i // 2
        pg = pre_g[i].reshape(1, D_MODEL)
        qg = post_g[i].reshape(1, D_MODEL)
        if i % 2 == 0:
            w = ssd_w_in[j]
            w_pad = jnp.pad(w, ((0, 0), (0, SSD_DT_PAD - 2 * SSD_HEADS))).astype(BF16)
            wdt_t = jnp.pad(w[:, SSD_DI + SSD_CONV_CH:].T,
                            ((0, SSD_DT_PAD - 2 * SSD_HEADS), (0, 0))).astype(BF16)
            z, xbc, dt, dtt = _ssd_in(r, pg, mod, i, w_pad, wdt_t)
            dtb_row = _pad_lanes_row(ssd_dt_bias[j])
            alog_row = _pad_lanes_row(ssd_a_log[j])
            yf, yb = _ssd_scan(
                xbc, dt, dtt, ssd_conv_w[j], ssd_conv_b[j].reshape(1, SSD_CONV_CH),
                dtb_row, dtb_row.reshape(SSD_DT_PAD, 1), alog_row, alog_row.reshape(SSD_DT_PAD, 1),
                jnp.repeat(ssd_d[j].astype(F32), SSD_HEADDIM).reshape(1, SSD_DI), expand)
            r = _ssd_out(yf, yb, z, ssd_norm_g[j].reshape(1, SSD_DI), ssd_w_out[j].astype(BF16),
                         r, qg, mod, i)
        else:
            qkv, gate = _na_in(r, pg, mod, i, na_w_in[j].astype(BF16))
            a = _na_attention(qkv, gate, _na_bias_table(na_rpb[j]))
            r = _na_out(a, na_w_out[j].astype(BF16), r, qg, mod, i)
    return r[CTX_LEN:][None]
```

```python
import functools
import math

import jax
import jax.numpy as jnp
import numpy as np
from jax import lax
from jax.experimental import pallas as pl
from jax.experimental.pallas import tpu as pltpu

D_MODEL = 1024
SEQ = 16384
DEPTH = 4
GRID_W = 64
GRID_H = SEQ // GRID_W
CTX_LEN = 256
ROWS = CTX_LEN + SEQ

SSD_DI = 2048
SSD_HEADDIM = 64
SSD_HEADS = 32
SSD_STATE = 128
SSD_GROUPS = 8
SSD_HEADS_PER_GROUP = SSD_HEADS // SSD_GROUPS
SSD_GROUP_W = SSD_DI // SSD_GROUPS
SSD_GN = SSD_GROUPS * SSD_STATE
SSD_CONV_CH = SSD_DI + 2 * SSD_GN
SSD_IN = SSD_DI + SSD_CONV_CH + 2 * SSD_HEADS
SSD_DT_PAD = 128

NA_HEADS = 16
NA_HEADDIM = 64
NA_DI = 1024
WIN_R = 8
WIN_C = 16

RMS_EPS = 1e-6
NEG_BIG = -1e30

LANES = 128
ROW_TILE = 256
SSD_CHUNK = 128
NA_QROWS = 4
NA_QBLK = NA_QROWS * GRID_W
VMEM_LIMIT = 56 * 1024 * 1024

F32 = jnp.float32
BF16 = jnp.bfloat16


def _silu(x):
    return x / (1.0 + jnp.exp(-x))


def _softplus(x):
    return jnp.maximum(x, 0.0) + jnp.log(1.0 + jnp.exp(-jnp.abs(x)))


def _dot(a, b):
    return jnp.dot(a, b, preferred_element_type=F32)


def _dot_nt(a, b):
    return lax.dot_general(a, b, (((1,), (1,)), ((), ())), preferred_element_type=F32)


def _split3(v):
    hi = v.astype(BF16)
    r1 = v - hi.astype(F32)
    mid = r1.astype(BF16)
    lo = (r1 - mid.astype(F32)).astype(BF16)
    return hi, mid, lo


def _dot_exact_rhs(sel, v):
    hi, mid, lo = _split3(v)
    return _dot(sel, hi) + _dot(sel, mid) + _dot(sel, lo)


def _dot_exact_lhs(v, sel):
    hi, mid, lo = _split3(v)
    return _dot(hi, sel) + _dot(mid, sel) + _dot(lo, sel)


def _mod_kernel(c_ref, w_ref, b_ref, o_ref):
    s = _silu(c_ref[...]).astype(BF16)
    o_ref[0] = _dot(s, w_ref[0].astype(BF16)) + b_ref[0]


def _modulation(cvec, ada_w, ada_b):
    tn = 1024
    return pl.pallas_call(
        _mod_kernel,
        name="modulation",
        grid=(DEPTH, 3 * D_MODEL // tn),
        in_specs=[
            pl.BlockSpec((8, D_MODEL), lambda i, j: (0, 0)),
            pl.BlockSpec((1, D_MODEL, tn), lambda i, j: (i, 0, j)),
            pl.BlockSpec((1, 1, tn), lambda i, j: (i, 0, j)),
        ],
        out_specs=pl.BlockSpec((1, 8, tn), lambda i, j: (i, 0, j)),
        out_shape=jax.ShapeDtypeStruct((DEPTH, 8, 3 * D_MODEL), F32),
        compiler_params=pltpu.CompilerParams(vmem_limit_bytes=VMEM_LIMIT),
    )(cvec, ada_w, ada_b.reshape(DEPTH, 1, 3 * D_MODEL))


def _mod_rows(mod_ref, is_ctx):
    m = mod_ref[0]
    row = jnp.where(is_ctx, m[1:2, :], m[0:1, :])
    return row[:, :D_MODEL], row[:, D_MODEL:2 * D_MODEL], row[:, 2 * D_MODEL:]


def _prenorm(x_ref, g_ref, mod_ref):
    x = x_ref[...]
    is_ctx = pl.program_id(0) * ROW_TILE < CTX_LEN
    shift, scale, _ = _mod_rows(mod_ref, is_ctx)
    y = x * lax.rsqrt(jnp.mean(x * x, axis=-1, keepdims=True) + RMS_EPS) * g_ref[...]
    return (y * (1.0 + scale) + shift).astype(BF16)


def _postnorm_residual(o, r_ref, g_ref, mod_ref, out_ref):
    is_ctx = pl.program_id(0) * ROW_TILE < CTX_LEN
    _, _, gate = _mod_rows(mod_ref, is_ctx)
    y = o * lax.rsqrt(jnp.mean(o * o, axis=-1, keepdims=True) + RMS_EPS) * g_ref[...]
    out_ref[...] = r_ref[...] + gate * y


def _row_spec(width):
    return pl.BlockSpec((ROW_TILE, width), lambda i: (i, 0))


def _const_spec(shape):
    nd = len(shape)
    return pl.BlockSpec(shape, lambda i: (0,) * nd)


def _layer_spec(shape, layer):
    nd = len(shape)
    return pl.BlockSpec((1,) + shape, lambda i: (layer,) + (0,) * nd)


_COL_CHUNK = 1024


def _ssd_in_kernel(x_ref, g_ref, mod_ref, w_ref, wdt_ref, z_ref, xbc_ref, dt_ref, dtt_ref):
    h = _prenorm(x_ref, g_ref, mod_ref)
    for c in range(0, SSD_DI, _COL_CHUNK):
        z_ref[:, c:c + _COL_CHUNK] = _dot(h, w_ref[:, c:c + _COL_CHUNK])
    for c in range(0, SSD_CONV_CH, _COL_CHUNK):
        xbc_ref[:, c:c + _COL_CHUNK] = _dot(h, w_ref[:, SSD_DI + c:SSD_DI + c + _COL_CHUNK])
    dt_ref[...] = _dot(h, w_ref[:, SSD_DI + SSD_CONV_CH:])
    dtt_ref[...] = _dot_nt(wdt_ref[...], h)


def _ssd_in(r, pre_g, mod, layer, w_pad, wdt_t):
    n_in = SSD_DI + SSD_CONV_CH + SSD_DT_PAD
    return pl.pallas_call(
        _ssd_in_kernel,
        name="ssd_in",
        grid=(ROWS // ROW_TILE,),
        in_specs=[
            _row_spec(D_MODEL),
            _const_spec((1, D_MODEL)),
            _layer_spec((8, 3 * D_MODEL), layer),
            pl.BlockSpec((D_MODEL, n_in), lambda i: (0, 0), pipeline_mode=pl.Buffered(1)),
            _const_spec((SSD_DT_PAD, D_MODEL)),
        ],
        out_specs=[
            _row_spec(SSD_DI),
            _row_spec(SSD_CONV_CH),
            _row_spec(SSD_DT_PAD),
            pl.BlockSpec((SSD_DT_PAD, ROW_TILE), lambda i: (0, i)),
        ],
        out_shape=[
            jax.ShapeDtypeStruct((ROWS, SSD_DI), F32),
            jax.ShapeDtypeStruct((ROWS, SSD_CONV_CH), F32),
            jax.ShapeDtypeStruct((ROWS, SSD_DT_PAD), F32),
            jax.ShapeDtypeStruct((SSD_DT_PAD, ROWS), F32),
        ],
        compiler_params=pltpu.CompilerParams(vmem_limit_bytes=VMEM_LIMIT),
    )(r, pre_g, mod, w_pad, wdt_t)


_N_CHUNKS = ROWS // SSD_CHUNK
_N_CTX_CHUNKS = CTX_LEN // SSD_CHUNK
_HALO = 8


def _fwd_chunk(i):
    return i


def _bwd_chunk(i):
    return jnp.where(i < _N_CTX_CHUNKS, _N_CTX_CHUNKS - 1 - i, _N_CHUNKS - 1 + _N_CTX_CHUNKS - i)


def _conv_silu(raw, prev8, next8, chunk, cw_ref, cb_ref):
    t = SSD_CHUNK
    first = (chunk == 0) | (chunk == _N_CTX_CHUNKS)
    last = (chunk == _N_CTX_CHUNKS - 1) | (chunk == _N_CHUNKS - 1)
    prev_row = jnp.where(first, 0.0, prev8[_HALO - 1:_HALO, :])
    next_row = jnp.where(last, 0.0, next8[0:1, :])
    rid = lax.broadcasted_iota(jnp.int32, (t, 1), 0)
    up = jnp.where(rid == 0, prev_row, pltpu.roll(raw, 1, axis=0))
    dn = jnp.where(rid == t - 1, next_row, pltpu.roll(raw, t - 1, axis=0))
    acc = cw_ref[0:1, :] * up + cw_ref[1:2, :] * raw + cw_ref[2:3, :] * dn + cb_ref[...]
    return _silu(acc)


def _ssd_direction(d, chunk, raw_ref, prev_ref, next_ref, dt_ref, dtt_ref,
                   cw_ref, cb_ref, dtb_row_ref, dtb_col_ref, alog_row_ref, alog_col_ref,
                   dskip_ref, exp_ref, state_ref, y_ref):
    t = SSD_CHUNK
    act = _conv_silu(raw_ref[...], prev_ref[...], next_ref[...], chunk, cw_ref, cb_ref)
    xs = act[:, :SSD_DI]
    xs_b = xs.astype(BF16)
    bm = act[:, SSD_DI:SSD_DI + SSD_GN]
    cm_b = act[:, SSD_DI + SSD_GN:].astype(BF16)

    dt_tok = _softplus(dt_ref[...] + dtb_row_ref[...])
    dt_head = _softplus(dtt_ref[...] + dtb_col_ref[...])
    a_tok = dt_tok * (-jnp.exp(alog_row_ref[...]))
    a_head = dt_head * (-jnp.exp(alog_col_ref[...]))

    li = lax.broadcasted_iota(jnp.int32, (t, t), 0)
    si = lax.broadcasted_iota(jnp.int32, (t, t), 1)
    lower = si <= li
    upper = si >= li
    mask = lower if d == 0 else upper
    tri = jnp.where(mask, 1.0, 0.0).astype(BF16)
    tri_t = jnp.where(upper if d == 0 else lower, 1.0, 0.0).astype(BF16)

    cum = _dot_exact_rhs(tri, a_tok)
    cum_t = _dot_exact_lhs(a_head, tri_t)
    tot = cum[t - 1:t, :] if d == 0 else cum[0:1, :]

    expand = exp_ref[d]
    e_in = _dot_exact_lhs(jnp.exp(cum), expand)
    w_out = _dot_exact_lhs(jnp.exp(tot - cum) * dt_tok, expand)
    e_tot = _dot_exact_lhs(jnp.broadcast_to(jnp.exp(tot), (8, LANES)), expand)[0:1, :]

    xw_b = (xs * w_out).astype(BF16)
    lane = lax.broadcasted_iota(jnp.int32, (1, LANES), 1)
    first_head = lane < SSD_HEADDIM

    for g in range(SSD_GROUPS):
        gs = slice(g * SSD_GROUP_W, (g + 1) * SSD_GROUP_W)
        ns = slice(g * SSD_STATE, (g + 1) * SSD_STATE)
        b_g = bm[:, ns]
        c_g = cm_b[:, ns]
        state = state_ref[d, g]
        y_off = _dot(c_g, state.astype(BF16)) * e_in[:, gs]
        state_ref[d, g] = state * e_tot[:, gs] + _dot(b_g.T.astype(BF16), xw_b[:, gs])
        cb = _dot_nt(c_g, b_g.astype(BF16))
        y_pairs = []
        for pair in range(SSD_HEADS_PER_GROUP // 2):
            ys = []
            for k in range(2):
                idx = d * SSD_HEADS + g * SSD_HEADS_PER_GROUP + 2 * pair + k
                seg = cum[:, idx:idx + 1] - cum_t[idx:idx + 1, :]
                decay = jnp.exp(jnp.where(mask, seg, NEG_BIG))
                m = (cb * decay * dt_head[idx:idx + 1, :]).astype(BF16)
                c0 = g * SSD_GROUP_W + pair * LANES
                ys.append(_dot(m, xs_b[:, c0:c0 + LANES]))
            y_pairs.append(jnp.where(first_head, ys[0], ys[1]))
        y = jnp.concatenate(y_pairs, axis=1) + y_off
        if d == 0:
            y = y + xs[:, gs] * dskip_ref[:, gs]
        y_ref[:, gs] = y


def _ssd_scan_kernel(raw_f, prev_f, next_f, dt_f, dtt_f, raw_b, prev_b, next_b, dt_b, dtt_b,
                     cw_ref, cb_ref, dtb_row_ref, dtb_col_ref, alog_row_ref, alog_col_ref,
                     dskip_ref, exp_ref, yf_ref, yb_ref, state_ref):
    i = pl.program_id(0)

    @pl.when(i == 0)
    def _():
        state_ref[...] = jnp.zeros_like(state_ref)

    params = (cw_ref, cb_ref, dtb_row_ref, dtb_col_ref, alog_row_ref, alog_col_ref,
              dskip_ref, exp_ref, state_ref)
    _ssd_direction(0, _fwd_chunk(i), raw_f, prev_f, next_f, dt_f, dtt_f, *params, yf_ref)
    _ssd_direction(1, _bwd_chunk(i), raw_b, prev_b, next_b, dt_b, dtt_b, *params, yb_ref)


def _ssd_scan(xbc, dt, dtt, conv_w, conv_b, dtb_row, dtb_col, alog_row, alog_col, dskip, expand):
    t = SSD_CHUNK
    per8 = t // _HALO
    last8 = ROWS // _HALO - 1

    def chunk_specs(order):
        return [
            pl.BlockSpec((t, SSD_CONV_CH), lambda i: (order(i), 0)),
            pl.BlockSpec((_HALO, SSD_CONV_CH), lambda i: (jnp.maximum(order(i) * per8 - 1, 0), 0)),
            pl.BlockSpec((_HALO, SSD_CONV_CH),
                         lambda i: (jnp.minimum((order(i) + 1) * per8, last8), 0)),
            pl.BlockSpec((t, SSD_DT_PAD), lambda i: (order(i), 0)),
            pl.BlockSpec((SSD_DT_PAD, t), lambda i: (0, order(i))),
        ]

    y_shape = jax.ShapeDtypeStruct((ROWS, SSD_DI), F32)
    return pl.pallas_call(
        _ssd_scan_kernel,
        name="ssd_scan",
        grid=(_N_CHUNKS,),
        in_specs=chunk_specs(_fwd_chunk) + chunk_specs(_bwd_chunk) + [
            _const_spec((3, SSD_CONV_CH)),
            _const_spec((1, SSD_CONV_CH)),
            _const_spec((1, SSD_DT_PAD)),
            _const_spec((SSD_DT_PAD, 1)),
            _const_spec((1, SSD_DT_PAD)),
            _const_spec((SSD_DT_PAD, 1)),
            _const_spec((1, SSD_DI)),
            _const_spec((2, SSD_DT_PAD, SSD_DI)),
        ],
        out_specs=[
            pl.BlockSpec((t, SSD_DI), lambda i: (_fwd_chunk(i), 0)),
            pl.BlockSpec((t, SSD_DI), lambda i: (_bwd_chunk(i), 0)),
        ],
        out_shape=[y_shape, y_shape],
        scratch_shapes=[pltpu.VMEM((2, SSD_GROUPS, SSD_STATE, SSD_GROUP_W), F32)],
        compiler_params=pltpu.CompilerParams(
            dimension_semantics=("arbitrary",), vmem_limit_bytes=VMEM_LIMIT),
    )(xbc, xbc, xbc, dt, dtt, xbc, xbc, xbc, dt, dtt,
      conv_w, conv_b, dtb_row, dtb_col, alog_row, alog_col, dskip, expand)


def _ssd_out_kernel(yf_ref, yb_ref, z_ref, ng_ref, w_ref, r_ref, pg_ref, mod_ref, out_ref):
    y = (yf_ref[...] + yb_ref[...]) * _silu(z_ref[...])
    parts = []
    for g in range(SSD_GROUPS):
        yg = y[:, g * SSD_GROUP_W:(g + 1) * SSD_GROUP_W]
        parts.append(yg * lax.rsqrt(jnp.mean(yg * yg, axis=-1, keepdims=True) + RMS_EPS))
    yn = (jnp.concatenate(parts, axis=1) * ng_ref[...]).astype(BF16)
    _postnorm_residual(_dot(yn, w_ref[...]), r_ref, pg_ref, mod_ref, out_ref)


def _ssd_out(yf, yb, z, norm_g, w_out, r, post_g, mod, layer):
    return pl.pallas_call(
        _ssd_out_kernel,
        name="ssd_out",
        grid=(ROWS // ROW_TILE,),
        in_specs=[
            _row_spec(SSD_DI), _row_spec(SSD_DI), _row_spec(SSD_DI),
            _const_spec((1, SSD_DI)),
            _const_spec((SSD_DI, D_MODEL)),
            _row_spec(D_MODEL),
            _const_spec((1, D_MODEL)),
            _layer_spec((8, 3 * D_MODEL), layer),
        ],
        out_specs=_row_spec(D_MODEL),
        out_shape=jax.ShapeDtypeStruct((ROWS, D_MODEL), F32),
        compiler_params=pltpu.CompilerParams(vmem_limit_bytes=VMEM_LIMIT),
    )(yf, yb, z, norm_g, w_out, r, post_g, mod)


def _na_in_kernel(x_ref, g_ref, mod_ref, w_ref, qkv_ref, gate_ref):
    h = _prenorm(x_ref, g_ref, mod_ref)
    scale = NA_HEADDIM ** -0.5
    qkv_ref[:, :NA_DI] = (_dot(h, w_ref[:, :NA_DI]) * scale).astype(BF16)
    for c in range(NA_DI, 3 * NA_DI, _COL_CHUNK):
        qkv_ref[:, c:c + _COL_CHUNK] = _dot(h, w_ref[:, c:c + _COL_CHUNK]).astype(BF16)
    gate_ref[...] = _dot(h, w_ref[:, 3 * NA_DI:])


def _na_in(r, pre_g, mod, layer, w_in):
    return pl.pallas_call(
        _na_in_kernel,
        name="na_in",
        grid=(ROWS // ROW_TILE,),
        in_specs=[
            _row_spec(D_MODEL),
            _const_spec((1, D_MODEL)),
            _layer_spec((8, 3 * D_MODEL), layer),
            pl.BlockSpec((D_MODEL, 4 * NA_DI), lambda i: (0, 0), pipeline_mode=pl.Buffered(1)),
        ],
        out_specs=[_row_spec(3 * NA_DI), _row_spec(NA_DI)],
        out_shape=[
            jax.ShapeDtypeStruct((ROWS, 3 * NA_DI), BF16),
            jax.ShapeDtypeStruct((ROWS, NA_DI), F32),
        ],
        compiler_params=pltpu.CompilerParams(vmem_limit_bytes=VMEM_LIMIT),
    )(r, pre_g, mod, w_in)


_NA_BLOCKS = ROWS // NA_QBLK
_NA_WIN_KEYS = 3 * NA_QBLK
_N_HEAD_PAIRS = NA_HEADS // 2
_VAR_TOP, _VAR_MID, _VAR_BOTTOM, _VAR_CTX = 0, 1, 2, 3


def _na_bias_table(rpb):
    n = GRID_W
    cols = np.arange(n)
    col_start = np.clip(cols - WIN_C // 2, 0, n - WIN_C)
    cvalid = (cols[None, :] >= col_start[:, None]) & (cols[None, :] < col_start[:, None] + WIN_C)

    rpb = rpb.astype(F32)
    edge = n - WIN_C
    ext = jnp.concatenate([jnp.repeat(rpb[..., :1], edge, axis=-1), rpb,
                           jnp.repeat(rpb[..., -1:], edge + 1, axis=-1)], axis=-1)
    skew = jnp.tile(ext, (1, 1, n))[..., :n * (2 * n - 1)].reshape(NA_HEADS, 2 * WIN_R - 1, n, 2 * n - 1)
    toep = jnp.where(cvalid, skew[..., n - 1:], NEG_BIG)
    masked = jnp.full((NA_HEADS, n, n), NEG_BIG, F32)

    def variant(r0):
        rows = []
        for qr in range(r0, r0 + NA_QROWS):
            start = min(max(qr - WIN_R // 2, 0), GRID_H - WIN_R)
            blocks = []
            for kr in range(r0 - NA_QROWS, r0 + 2 * NA_QROWS):
                valid = start <= kr < start + WIN_R
                blocks.append(toep[:, kr - qr + WIN_R - 1] if valid else masked)
            rows.append(jnp.concatenate(blocks, axis=-1))
        return jnp.concatenate(rows, axis=1)

    ctx_variant = jnp.full((NA_HEADS, NA_QBLK, _NA_WIN_KEYS), NEG_BIG, F32)
    return jnp.stack([variant(0), variant(2 * NA_QROWS), variant(GRID_H - NA_QROWS), ctx_variant])


def _na_kernel(q_ref, kp_ref, kc_ref, kn_ref, kx_ref, vp_ref, vc_ref, vn_ref, vx_ref,
               gate_ref, bias_ref, o_ref):
    q = q_ref[...]
    lane = lax.broadcasted_iota(jnp.int32, (1, LANES), 1)
    first_head = lane < NA_HEADDIM
    keys = (kp_ref[...], kc_ref[...], kn_ref[...], kx_ref[...])
    vals = (vp_ref[...], vc_ref[...], vn_ref[...], vx_ref[...])
    outs = []
    for e in range(2):
        qe = jnp.where(first_head if e == 0 else ~first_head, q, jnp.zeros_like(q))
        bias = bias_ref[0, e]
        s = [_dot_nt(qe, k) for k in keys]
        for j in range(3):
            s[j] = s[j] + bias[:, j * NA_QBLK:(j + 1) * NA_QBLK]
        m = functools.reduce(jnp.maximum, [jnp.max(x, axis=-1, keepdims=True) for x in s])
        p = [jnp.exp(x - m) for x in s]
        denom = functools.reduce(jnp.add, [jnp.sum(x, axis=-1, keepdims=True) for x in p])
        inv = 1.0 / denom
        o = functools.reduce(jnp.add, [_dot((x * inv).astype(BF16), v) for x, v in zip(p, vals)])
        outs.append(o)
    o = jnp.where(first_head, outs[0], outs[1])
    o_ref[...] = (o * _silu(gate_ref[...])).astype(BF16)


def _na_attention(qkv, gate, bias):
    nq = NA_DI // LANES

    def variant(rb):
        return jnp.where(rb == 0, _VAR_CTX,
                         jnp.where(rb == 1, _VAR_TOP,
                                   jnp.where(rb == _NA_BLOCKS - 1, _VAR_BOTTOM, _VAR_MID)))

    def blk(row_fn, section):
        return pl.BlockSpec((NA_QBLK, LANES), lambda hp, rb: (row_fn(rb), section * nq + hp))

    prev_rb = lambda rb: jnp.maximum(rb - 1, 0)
    next_rb = lambda rb: jnp.minimum(rb + 1, _NA_BLOCKS - 1)
    same_rb = lambda rb: rb
    ctx_rb = lambda rb: 0
    return pl.pallas_call(
        _na_kernel,
        name="na_attention",
        grid=(_N_HEAD_PAIRS, _NA_BLOCKS),
        in_specs=[
            blk(same_rb, 0),
            blk(prev_rb, 1), blk(same_rb, 1), blk(next_rb, 1), blk(ctx_rb, 1),
            blk(prev_rb, 2), blk(same_rb, 2), blk(next_rb, 2), blk(ctx_rb, 2),
            pl.BlockSpec((NA_QBLK, LANES), lambda hp, rb: (rb, hp)),
            pl.BlockSpec((1, 2, NA_QBLK, _NA_WIN_KEYS), lambda hp, rb: (variant(rb), hp, 0, 0)),
        ],
        out_specs=pl.BlockSpec((NA_QBLK, LANES), lambda hp, rb: (rb, hp)),
        out_shape=jax.ShapeDtypeStruct((ROWS, NA_DI), BF16),
        compiler_params=pltpu.CompilerParams(vmem_limit_bytes=VMEM_LIMIT),
    )(qkv, qkv, qkv, qkv, qkv, qkv, qkv, qkv, qkv, gate, bias)


def _na_out_kernel(a_ref, w_ref, r_ref, pg_ref, mod_ref, out_ref):
    _postnorm_residual(_dot(a_ref[...], w_ref[...]), r_ref, pg_ref, mod_ref, out_ref)


def _na_out(a, w_out, r, post_g, mod, layer):
    return pl.pallas_call(
        _na_out_kernel,
        name="na_out",
        grid=(ROWS // ROW_TILE,),
        in_specs=[
            _row_spec(NA_DI),
            _const_spec((NA_DI, D_MODEL)),
            _row_spec(D_MODEL),
            _const_spec((1, D_MODEL)),
            _layer_spec((8, 3 * D_MODEL), layer),
        ],
        out_specs=_row_spec(D_MODEL),
        out_shape=jax.ShapeDtypeStruct((ROWS, D_MODEL), F32),
        compiler_params=pltpu.CompilerParams(vmem_limit_bytes=VMEM_LIMIT),
    )(a, w_out, r, post_g, mod)


def _pad_lanes_row(v):
    flat = v.astype(F32).reshape(1, 2 * SSD_HEADS)
    return jnp.pad(flat, ((0, 0), (0, SSD_DT_PAD - 2 * SSD_HEADS)))


def _head_expand():
    rows = jnp.arange(SSD_DT_PAD)[:, None]
    head_of_col = jnp.arange(SSD_DI)[None, :] // SSD_HEADDIM
    return jnp.stack([(rows == head_of_col + d * SSD_HEADS) for d in range(2)]).astype(BF16)


def kernel(x, c, ctx, c_ctx, ada_w, ada_b, pre_g, post_g, ssd_w_in, ssd_conv_w, ssd_conv_b,
           ssd_dt_bias, ssd_a_log, ssd_d, ssd_norm_g, ssd_w_out, na_w_in, na_rpb, na_w_out):
    assert x.shape == (1, SEQ, D_MODEL) and ctx.shape == (1, CTX_LEN, D_MODEL)
    r = jnp.concatenate([ctx[0], x[0]], axis=0)
    cvec = jnp.zeros((8, D_MODEL), F32).at[0].set(c[0]).at[1].set(c_ctx)
    mod = _modulation(cvec, ada_w, ada_b)
    expand = _head_expand()

    for i in range(DEPTH):
        j = i // 2
        pg = pre_g[i].reshape(1, D_MODEL)
        qg = post_g[i].reshape(1, D_MODEL)
        if i % 2 == 0:
            w = ssd_w_in[j]
            w_pad = jnp.pad(w, ((0, 0), (0, SSD_DT_PAD - 2 * SSD_HEADS))).astype(BF16)
            wdt_t = jnp.pad(w[:, SSD_DI + SSD_CONV_CH:].T,
                            ((0, SSD_DT_PAD - 2 * SSD_HEADS), (0, 0))).astype(BF16)
            z, xbc, dt, dtt = _ssd_in(r, pg, mod, i, w_pad, wdt_t)
            dtb_row = _pad_lanes_row(ssd_dt_bias[j])
            alog_row = _pad_lanes_row(ssd_a_log[j])
            yf, yb = _ssd_scan(
                xbc, dt, dtt, ssd_conv_w[j], ssd_conv_b[j].reshape(1, SSD_CONV_CH),
                dtb_row, dtb_row.reshape(SSD_DT_PAD, 1), alog_row, alog_row.reshape(SSD_DT_PAD, 1),
                jnp.repeat(ssd_d[j].astype(F32), SSD_HEADDIM).reshape(1, SSD_DI), expand)
            r = _ssd_out(yf, yb, z, ssd_norm_g[j].reshape(1, SSD_DI), ssd_w_out[j].astype(BF16),
                         r, qg, mod, i)
        else:
            qkv, gate = _na_in(r, pg, mod, i, na_w_in[j].astype(BF16))
            a = _na_attention(qkv, gate, _na_bias_table(na_rpb[j]))
            r = _na_out(a, na_w_out[j].astype(BF16), r, qg, mod, i)
    return r[CTX_LEN:][None]
```

```python
import functools

import jax
import jax.numpy as jnp
import numpy as np
from jax import lax
from jax.experimental import pallas as pl
from jax.experimental.pallas import tpu as pltpu

D_MODEL = 1024
SEQ = 16384
DEPTH = 4
GRID_W = 64
GRID_H = SEQ // GRID_W
CTX_LEN = 256
ROWS = CTX_LEN + SEQ

SSD_DI = 2048
SSD_HEADDIM = 64
SSD_HEADS = 32
SSD_STATE = 128
SSD_GROUPS = 8
SSD_HEADS_PER_GROUP = SSD_HEADS // SSD_GROUPS
SSD_GROUP_W = SSD_DI // SSD_GROUPS
SSD_GN = SSD_GROUPS * SSD_STATE
SSD_CONV_CH = SSD_DI + 2 * SSD_GN
SSD_IN = SSD_DI + SSD_CONV_CH + 2 * SSD_HEADS
SSD_DT_PAD = 128

NA_HEADS = 16
NA_HEADDIM = 64
NA_DI = 1024
WIN_R = 8
WIN_C = 16

RMS_EPS = 1e-6
NEG_BIG = -1e30

LANES = 128
ROW_TILE = 256
SSD_CHUNK = 128
NA_QROWS = 4
NA_QBLK = NA_QROWS * GRID_W
NA_PAIRS_PER_STEP = 2
VMEM_LIMIT = 56 * 1024 * 1024

F32 = jnp.float32
BF16 = jnp.bfloat16


def _silu(x):
    return x / (1.0 + jnp.exp(-x))


def _softplus(x):
    return jnp.maximum(x, 0.0) + jnp.log(1.0 + jnp.exp(-jnp.abs(x)))


def _dot(a, b):
    return jnp.dot(a, b, preferred_element_type=F32)


def _dot_nt(a, b):
    return lax.dot_general(a, b, (((1,), (1,)), ((), ())), preferred_element_type=F32)


def _dot_split2(v, sel):
    hi = v.astype(BF16)
    lo = (v - hi.astype(F32)).astype(BF16)
    return _dot(hi, sel) + _dot(lo, sel)


def _split3(v):
    hi = v.astype(BF16)
    r1 = v - hi.astype(F32)
    mid = r1.astype(BF16)
    lo = (r1 - mid.astype(F32)).astype(BF16)
    return hi, mid, lo


def _dot_exact_rhs(sel, v):
    hi, mid, lo = _split3(v)
    return _dot(sel, hi) + _dot(sel, mid) + _dot(sel, lo)


def _dot_exact_lhs(v, sel):
    hi, mid, lo = _split3(v)
    return _dot(hi, sel) + _dot(mid, sel) + _dot(lo, sel)


def _mod_kernel(c_ref, w_ref, b_ref, o_ref):
    s = _silu(c_ref[...]).astype(BF16)
    o_ref[0] = _dot(s, w_ref[0].astype(BF16)) + b_ref[0]


def _modulation(cvec, ada_w, ada_b):
    tn = 1024
    return pl.pallas_call(
        _mod_kernel,
        name="modulation",
        grid=(DEPTH, 3 * D_MODEL // tn),
        in_specs=[
            pl.BlockSpec((8, D_MODEL), lambda i, j: (0, 0)),
            pl.BlockSpec((1, D_MODEL, tn), lambda i, j: (i, 0, j)),
            pl.BlockSpec((1, 1, tn), lambda i, j: (i, 0, j)),
        ],
        out_specs=pl.BlockSpec((1, 8, tn), lambda i, j: (i, 0, j)),
        out_shape=jax.ShapeDtypeStruct((DEPTH, 8, 3 * D_MODEL), F32),
        compiler_params=pltpu.CompilerParams(vmem_limit_bytes=VMEM_LIMIT),
    )(cvec, ada_w, ada_b.reshape(DEPTH, 1, 3 * D_MODEL))


def _mod_rows(mod_ref, is_ctx):
    m = mod_ref[0]
    row = jnp.where(is_ctx, m[1:2, :], m[0:1, :])
    return row[:, :D_MODEL], row[:, D_MODEL:2 * D_MODEL], row[:, 2 * D_MODEL:]


def _is_ctx_tile():
    return pl.program_id(0) * ROW_TILE < CTX_LEN


def _prenorm(x, g_ref, mod_ref):
    is_ctx = _is_ctx_tile()
    shift, scale, _ = _mod_rows(mod_ref, is_ctx)
    y = x * lax.rsqrt(jnp.mean(x * x, axis=-1, keepdims=True) + RMS_EPS) * g_ref[...]
    return (y * (1.0 + scale) + shift).astype(BF16)


def _postnorm_residual(o, r, g_ref, mod_ref, out_ref):
    _, _, gate = _mod_rows(mod_ref, _is_ctx_tile())
    y = o * lax.rsqrt(jnp.mean(o * o, axis=-1, keepdims=True) + RMS_EPS) * g_ref[...]
    out_ref[...] = r + gate * y


def _row_spec(width):
    return pl.BlockSpec((ROW_TILE, width), lambda i: (i, 0))


def _const_spec(shape):
    nd = len(shape)
    return pl.BlockSpec(shape, lambda i: (0,) * nd)


def _layer_spec(shape, layer):
    nd = len(shape)
    return pl.BlockSpec((1,) + shape, lambda i: (layer,) + (0,) * nd)


_COL_CHUNK = 1024
_HALO = 8
_N_ROW_TILES = ROWS // ROW_TILE
_CTX_ROW_TILES = CTX_LEN // ROW_TILE


def _ssd_in_kernel(x_ref, xp_ref, xn_ref, g_ref, mod_ref, w_ref, wdt_ref, cw_ref, cb_ref,
                   dtb_row_ref, dtb_col_ref, z_ref, xs_ref, bc_ref, dt_ref, dtt_ref):
    i = pl.program_id(0)
    h = _prenorm(x_ref[...], g_ref, mod_ref)
    h_ext = jnp.concatenate([_prenorm(xp_ref[...], g_ref, mod_ref), h,
                             _prenorm(xn_ref[...], g_ref, mod_ref)], axis=0)
    first = (i == 0) | (i == _CTX_ROW_TILES)
    last = (i == _CTX_ROW_TILES - 1) | (i == _N_ROW_TILES - 1)
    rid = lax.broadcasted_iota(jnp.int32, (ROW_TILE, 1), 0)
    keep_up = jnp.logical_not(first & (rid == 0))
    keep_dn = jnp.logical_not(last & (rid == ROW_TILE - 1))
    n_ext = ROW_TILE + 2 * _HALO

    for c in range(0, SSD_DI, _COL_CHUNK):
        z_ref[:, c:c + _COL_CHUNK] = _dot(h, w_ref[:, c:c + _COL_CHUNK])
    for c in range(0, SSD_CONV_CH, _COL_CHUNK):
        cs = slice(c, c + _COL_CHUNK)
        t = _dot(h_ext, w_ref[:, SSD_DI + c:SSD_DI + c + _COL_CHUNK])
        raw = t[_HALO:_HALO + ROW_TILE]
        up = jnp.where(keep_up, pltpu.roll(t, 1, axis=0)[_HALO:_HALO + ROW_TILE], 0.0)
        dn = jnp.where(keep_dn, pltpu.roll(t, n_ext - 1, axis=0)[_HALO:_HALO + ROW_TILE], 0.0)
        act = _silu(cw_ref[0:1, cs] * up + cw_ref[1:2, cs] * raw + cw_ref[2:3, cs] * dn
                    + cb_ref[:, cs])
        if c < SSD_DI:
            xs_ref[:, cs] = act
        else:
            bc_ref[:, c - SSD_DI:c - SSD_DI + _COL_CHUNK] = act.astype(BF16)
    dt_ref[...] = _softplus(_dot(h, w_ref[:, SSD_DI + SSD_CONV_CH:]) + dtb_row_ref[...])
    dtt_ref[...] = _softplus(_dot_nt(wdt_ref[...], h) + dtb_col_ref[...])


def _ssd_in(r, pre_g, mod, layer, w_pad, wdt_t, conv_w, conv_b, dtb_row, dtb_col):
    n_in = SSD_DI + SSD_CONV_CH + SSD_DT_PAD
    per8 = ROW_TILE // _HALO
    last8 = ROWS // _HALO - 1
    return pl.pallas_call(
        _ssd_in_kernel,
        name="ssd_in",
        grid=(_N_ROW_TILES,),
        in_specs=[
            _row_spec(D_MODEL),
            pl.BlockSpec((_HALO, D_MODEL), lambda i: (jnp.maximum(i * per8 - 1, 0), 0)),
            pl.BlockSpec((_HALO, D_MODEL), lambda i: (jnp.minimum((i + 1) * per8, last8), 0)),
            _const_spec((1, D_MODEL)),
            _layer_spec((8, 3 * D_MODEL), layer),
            pl.BlockSpec((D_MODEL, n_in), lambda i: (0, 0), pipeline_mode=pl.Buffered(1)),
            _const_spec((SSD_DT_PAD, D_MODEL)),
            _const_spec((3, SSD_CONV_CH)),
            _const_spec((1, SSD_CONV_CH)),
            _const_spec((1, SSD_DT_PAD)),
            _const_spec((SSD_DT_PAD, 1)),
        ],
        out_specs=[
            _row_spec(SSD_DI),
            _row_spec(SSD_DI),
            _row_spec(2 * SSD_GN),
            _row_spec(SSD_DT_PAD),
            pl.BlockSpec((SSD_DT_PAD, ROW_TILE), lambda i: (0, i)),
        ],
        out_shape=[
            jax.ShapeDtypeStruct((ROWS, SSD_DI), F32),
            jax.ShapeDtypeStruct((ROWS, SSD_DI), F32),
            jax.ShapeDtypeStruct((ROWS, 2 * SSD_GN), BF16),
            jax.ShapeDtypeStruct((ROWS, SSD_DT_PAD), F32),
            jax.ShapeDtypeStruct((SSD_DT_PAD, ROWS), F32),
        ],
        compiler_params=pltpu.CompilerParams(vmem_limit_bytes=VMEM_LIMIT),
    )(r, r, r, pre_g, mod, w_pad, wdt_t, conv_w, conv_b, dtb_row, dtb_col)


_N_CHUNKS = ROWS // SSD_CHUNK
_N_CTX_CHUNKS = CTX_LEN // SSD_CHUNK


def _fwd_chunk(i):
    return i


def _bwd_chunk(i):
    return jnp.where(i < _N_CTX_CHUNKS, _N_CTX_CHUNKS - 1 - i, _N_CHUNKS - 1 + _N_CTX_CHUNKS - i)


def _ssd_direction(d, xs_ref, bc_ref, dt_ref, dtt_ref, alog_row_ref, alog_col_ref,
                   dskip_ref, exp_ref, state_ref, y_ref):
    t = SSD_CHUNK
    xs = xs_ref[...]
    xs_b = xs.astype(BF16)

    dt_head = dtt_ref[...]
    a_tok = dt_ref[...] * (-jnp.exp(alog_row_ref[...]))
    a_head = dt_head * (-jnp.exp(alog_col_ref[...]))

    li = lax.broadcasted_iota(jnp.int32, (t, t), 0)
    si = lax.broadcasted_iota(jnp.int32, (t, t), 1)
    lower = si <= li
    upper = si >= li
    mask = lower if d == 0 else upper
    tri = jnp.where(mask, 1.0, 0.0).astype(BF16)
    tri_t = jnp.where(upper if d == 0 else lower, 1.0, 0.0).astype(BF16)

    cum = _dot_exact_rhs(tri, a_tok)
    cum_t = _dot_exact_lhs(a_head, tri_t)
    tot = cum[t - 1:t, :] if d == 0 else cum[0:1, :]

    tot_col = cum_t[:, t - 1:t] if d == 0 else cum_t[:, 0:1]
    w_head = jnp.exp(tot_col - cum_t) * dt_head
    e_tot = _dot_split2(jnp.broadcast_to(jnp.exp(tot), (8, LANES)), exp_ref[d])[0:1, :]

    lane = lax.broadcasted_iota(jnp.int32, (1, LANES), 1)
    first_head = lane < SSD_HEADDIM

    for g in range(SSD_GROUPS):
        b_g = bc_ref[:, g * SSD_STATE:(g + 1) * SSD_STATE]
        c_g = bc_ref[:, SSD_GN + g * SSD_STATE:SSD_GN + (g + 1) * SSD_STATE]
        c_f = c_g.astype(F32)
        bt_g = b_g.astype(F32).T
        cb = _dot_nt(c_g, b_g)
        for pair in range(SSD_HEADS_PER_GROUP // 2):
            cs = slice(g * SSD_GROUP_W + pair * LANES, g * SSD_GROUP_W + (pair + 1) * LANES)
            ps = slice(pair * LANES, (pair + 1) * LANES)
            m, ce, btw = [], [], []
            for k in range(2):
                idx = d * SSD_HEADS + g * SSD_HEADS_PER_GROUP + 2 * pair + k
                col = jnp.broadcast_to(cum[:, idx:idx + 1], (t, t))
                decay = jnp.exp(jnp.where(mask, col - cum_t[idx:idx + 1, :], NEG_BIG))
                m.append((cb * decay * dt_head[idx:idx + 1, :]).astype(BF16))
                ce.append((c_f * jnp.exp(col)).astype(BF16))
                btw.append((bt_g * w_head[idx:idx + 1, :]).astype(BF16))
            x_pair = xs_b[:, cs]
            zero_x = jnp.zeros_like(x_pair)
            x_stack = jnp.concatenate([jnp.where(first_head, x_pair, zero_x),
                                       jnp.where(first_head, zero_x, x_pair)], axis=0)
            state = state_ref[d, g, :, ps]
            s_stack = jnp.concatenate([jnp.where(first_head, state, 0.0),
                                       jnp.where(first_head, 0.0, state)], axis=0).astype(BF16)
            y = _dot(jnp.concatenate(m + ce, axis=1), jnp.concatenate([x_stack, s_stack], axis=0))
            state_ref[d, g, :, ps] = state * e_tot[:, cs] + _dot(jnp.concatenate(btw, axis=1), x_stack)
            if d == 0:
                y = y + xs[:, cs] * dskip_ref[:, cs]
            y_ref[:, cs] = y


def _ssd_scan_kernel(xs_f, bc_f, dt_f, dtt_f, xs_b, bc_b, dt_b, dtt_b,
                     alog_row_ref, alog_col_ref, dskip_ref, exp_ref, yf_ref, yb_ref, state_ref):
    @pl.when(pl.program_id(0) == 0)
    def _():
        state_ref[...] = jnp.zeros_like(state_ref)

    params = (alog_row_ref, alog_col_ref, dskip_ref, exp_ref, state_ref)
    _ssd_direction(0, xs_f, bc_f, dt_f, dtt_f, *params, yf_ref)
    _ssd_direction(1, xs_b, bc_b, dt_b, dtt_b, *params, yb_ref)


def _ssd_scan(xs, bc, dt, dtt, alog_row, alog_col, dskip, expand):
    t = SSD_CHUNK

    def chunk_specs(order):
        return [
            pl.BlockSpec((t, SSD_DI), lambda i: (order(i), 0)),
            pl.BlockSpec((t, 2 * SSD_GN), lambda i: (order(i), 0)),
            pl.BlockSpec((t, SSD_DT_PAD), lambda i: (order(i), 0)),
            pl.BlockSpec((SSD_DT_PAD, t), lambda i: (0, order(i))),
        ]

    y_shape = jax.ShapeDtypeStruct((ROWS, SSD_DI), F32)
    return pl.pallas_call(
        _ssd_scan_kernel,
        name="ssd_scan",
        grid=(_N_CHUNKS,),
        in_specs=chunk_specs(_fwd_chunk) + chunk_specs(_bwd_chunk) + [
            _const_spec((1, SSD_DT_PAD)),
            _const_spec((SSD_DT_PAD, 1)),
            _const_spec((1, SSD_DI)),
            _const_spec((2, SSD_DT_PAD, SSD_DI)),
        ],
        out_specs=[
            pl.BlockSpec((t, SSD_DI), lambda i: (_fwd_chunk(i), 0)),
            pl.BlockSpec((t, SSD_DI), lambda i: (_bwd_chunk(i), 0)),
        ],
        out_shape=[y_shape, y_shape],
        scratch_shapes=[pltpu.VMEM((2, SSD_GROUPS, SSD_STATE, SSD_GROUP_W), F32)],
        compiler_params=pltpu.CompilerParams(
            dimension_semantics=("arbitrary",), vmem_limit_bytes=VMEM_LIMIT),
    )(xs, bc, dt, dtt, xs, bc, dt, dtt, alog_row, alog_col, dskip, expand)


def _ssd_out_kernel(yf_ref, yb_ref, z_ref, ng_ref, w_ref, r_ref, pg_ref, mod_ref, out_ref):
    y = (yf_ref[...] + yb_ref[...]) * _silu(z_ref[...])
    parts = []
    for g in range(SSD_GROUPS):
        yg = y[:, g * SSD_GROUP_W:(g + 1) * SSD_GROUP_W]
        parts.append(yg * lax.rsqrt(jnp.mean(yg * yg, axis=-1, keepdims=True) + RMS_EPS))
    yn = (jnp.concatenate(parts, axis=1) * ng_ref[...]).astype(BF16)
    _postnorm_residual(_dot(yn, w_ref[...]), r_ref[...], pg_ref, mod_ref, out_ref)


def _ssd_out(yf, yb, z, norm_g, w_out, r, post_g, mod, layer):
    return pl.pallas_call(
        _ssd_out_kernel,
        name="ssd_out",
        grid=(ROWS // ROW_TILE,),
        in_specs=[
            _row_spec(SSD_DI), _row_spec(SSD_DI), _row_spec(SSD_DI),
            _const_spec((1, SSD_DI)),
            _const_spec((SSD_DI, D_MODEL)),
            _row_spec(D_MODEL),
            _const_spec((1, D_MODEL)),
            _layer_spec((8, 3 * D_MODEL), layer),
        ],
        out_specs=_row_spec(D_MODEL),
        out_shape=jax.ShapeDtypeStruct((ROWS, D_MODEL), F32),
        compiler_params=pltpu.CompilerParams(vmem_limit_bytes=VMEM_LIMIT),
    )(yf, yb, z, norm_g, w_out, r, post_g, mod)


def _na_in_kernel(x_ref, g_ref, mod_ref, w_ref, qkv_ref, gate_ref):
    h = _prenorm(x_ref[...], g_ref, mod_ref)
    scale = NA_HEADDIM ** -0.5
    qkv_ref[:, :NA_DI] = (_dot(h, w_ref[:, :NA_DI]) * scale).astype(BF16)
    for c in range(NA_DI, 3 * NA_DI, _COL_CHUNK):
        qkv_ref[:, c:c + _COL_CHUNK] = _dot(h, w_ref[:, c:c + _COL_CHUNK]).astype(BF16)
    gate_ref[...] = _dot(h, w_ref[:, 3 * NA_DI:])


def _na_in(r, pre_g, mod, layer, w_in):
    return pl.pallas_call(
        _na_in_kernel,
        name="na_in",
        grid=(ROWS // ROW_TILE,),
        in_specs=[
            _row_spec(D_MODEL),
            _const_spec((1, D_MODEL)),
            _layer_spec((8, 3 * D_MODEL), layer),
            pl.BlockSpec((D_MODEL, 4 * NA_DI), lambda i: (0, 0), pipeline_mode=pl.Buffered(1)),
        ],
        out_specs=[_row_spec(3 * NA_DI), _row_spec(NA_DI)],
        out_shape=[
            jax.ShapeDtypeStruct((ROWS, 3 * NA_DI), BF16),
            jax.ShapeDtypeStruct((ROWS, NA_DI), F32),
        ],
        compiler_params=pltpu.CompilerParams(vmem_limit_bytes=VMEM_LIMIT),
    )(r, pre_g, mod, w_in)


_NA_BLOCKS = ROWS // NA_QBLK
_NA_WIN_KEYS = 3 * NA_QBLK
_VAR_TOP, _VAR_MID, _VAR_BOTTOM, _VAR_CTX = 0, 1, 2, 3


def _na_bias_table(rpb):
    n = GRID_W
    cols = np.arange(n)
    col_start = np.clip(cols - WIN_C // 2, 0, n - WIN_C)
    cvalid = (cols[None, :] >= col_start[:, None]) & (cols[None, :] < col_start[:, None] + WIN_C)

    rpb = rpb.astype(F32)
    edge = n - WIN_C
    ext = jnp.concatenate([jnp.repeat(rpb[..., :1], edge, axis=-1), rpb,
                           jnp.repeat(rpb[..., -1:], edge + 1, axis=-1)], axis=-1)
    skew = jnp.tile(ext, (1, 1, n))[..., :n * (2 * n - 1)].reshape(NA_HEADS, 2 * WIN_R - 1, n, 2 * n - 1)
    toep = jnp.where(cvalid, skew[..., n - 1:], NEG_BIG)
    masked = jnp.full((NA_HEADS, n, n), NEG_BIG, F32)

    def variant(r0):
        rows = []
        for qr in range(r0, r0 + NA_QROWS):
            start = min(max(qr - WIN_R // 2, 0), GRID_H - WIN_R)
            blocks = []
            for kr in range(r0 - NA_QROWS, r0 + 2 * NA_QROWS):
                valid = start <= kr < start + WIN_R
                blocks.append(toep[:, kr - qr + WIN_R - 1] if valid else masked)
            rows.append(jnp.concatenate(blocks, axis=-1))
        return jnp.concatenate(rows, axis=1)

    ctx_variant = jnp.full((NA_HEADS, NA_QBLK, _NA_WIN_KEYS), NEG_BIG, F32)
    return jnp.stack([variant(0), variant(2 * NA_QROWS), variant(GRID_H - NA_QROWS), ctx_variant])


def _na_head_pair(q, keys, vals, bias_ref, pair):
    lane = lax.broadcasted_iota(jnp.int32, (1, LANES), 1)
    first_head = lane < NA_HEADDIM
    outs = []
    for e in range(2):
        qe = jnp.where(first_head if e == 0 else ~first_head, q, jnp.zeros_like(q))
        s = [_dot_nt(qe, k) for k in keys]
        for j in range(3):
            s[j] = s[j] + bias_ref[0, 2 * pair + e, :, j * NA_QBLK:(j + 1) * NA_QBLK]
        m = functools.reduce(jnp.maximum, [jnp.max(x, axis=-1, keepdims=True) for x in s])
        p = [jnp.exp(x - m) for x in s]
        denom = functools.reduce(jnp.add, [jnp.sum(x, axis=-1, keepdims=True) for x in p])
        inv = 1.0 / denom
        outs.append(functools.reduce(
            jnp.add, [_dot((x * inv).astype(BF16), v) for x, v in zip(p, vals)]))
    return jnp.where(first_head, outs[0], outs[1])


def _na_kernel(q_ref, kp_ref, kc_ref, kn_ref, kx_ref, vp_ref, vc_ref, vn_ref, vx_ref,
               gate_ref, bias_ref, o_ref):
    for pair in range(NA_PAIRS_PER_STEP):
        ls = slice(pair * LANES, (pair + 1) * LANES)
        keys = (kp_ref[:, ls], kc_ref[:, ls], kn_ref[:, ls], kx_ref[:, ls])
        vals = (vp_ref[:, ls], vc_ref[:, ls], vn_ref[:, ls], vx_ref[:, ls])
        o = _na_head_pair(q_ref[:, ls], keys, vals, bias_ref, pair)
        o_ref[:, ls] = (o * _silu(gate_ref[:, ls])).astype(BF16)


def _na_attention(qkv, gate, bias):
    width = NA_PAIRS_PER_STEP * LANES
    nq = NA_DI // width

    def variant(rb):
        return jnp.where(rb == 0, _VAR_CTX,
                         jnp.where(rb == 1, _VAR_TOP,
                                   jnp.where(rb == _NA_BLOCKS - 1, _VAR_BOTTOM, _VAR_MID)))

    def blk(row_fn, section):
        return pl.BlockSpec((NA_QBLK, width), lambda hq, rb: (row_fn(rb), section * nq + hq))

    prev_rb = lambda rb: jnp.maximum(rb - 1, 0)
    next_rb = lambda rb: jnp.minimum(rb + 1, _NA_BLOCKS - 1)
    same_rb = lambda rb: rb
    ctx_rb = lambda rb: 0
    return pl.pallas_call(
        _na_kernel,
        name="na_attention",
        grid=(nq, _NA_BLOCKS),
        in_specs=[
            blk(same_rb, 0),
            blk(prev_rb, 1), blk(same_rb, 1), blk(next_rb, 1), blk(ctx_rb, 1),
            blk(prev_rb, 2), blk(same_rb, 2), blk(next_rb, 2), blk(ctx_rb, 2),
            pl.BlockSpec((NA_QBLK, width), lambda hq, rb: (rb, hq)),
            pl.BlockSpec((1, 2 * NA_PAIRS_PER_STEP, NA_QBLK, _NA_WIN_KEYS),
                         lambda hq, rb: (variant(rb), hq, 0, 0)),
        ],
        out_specs=pl.BlockSpec((NA_QBLK, width), lambda hq, rb: (rb, hq)),
        out_shape=jax.ShapeDtypeStruct((ROWS, NA_DI), BF16),
        compiler_params=pltpu.CompilerParams(vmem_limit_bytes=VMEM_LIMIT),
    )(qkv, qkv, qkv, qkv, qkv, qkv, qkv, qkv, qkv, gate, bias)


def _na_out_kernel(a_ref, w_ref, r_ref, pg_ref, mod_ref, out_ref):
    _postnorm_residual(_dot(a_ref[...], w_ref[...]), r_ref[...], pg_ref, mod_ref, out_ref)


def _na_out(a, w_out, r, post_g, mod, layer, latent_only):
    if latent_only:
        n_ctx = CTX_LEN // ROW_TILE
        out_spec = pl.BlockSpec((ROW_TILE, D_MODEL), lambda i: (jnp.maximum(i - n_ctx, 0), 0))
        out_rows = SEQ
    else:
        out_spec, out_rows = _row_spec(D_MODEL), ROWS
    return pl.pallas_call(
        _na_out_kernel,
        name="na_out",
        grid=(ROWS // ROW_TILE,),
        in_specs=[
            _row_spec(NA_DI),
            _const_spec((NA_DI, D_MODEL)),
            _row_spec(D_MODEL),
            _const_spec((1, D_MODEL)),
            _layer_spec((8, 3 * D_MODEL), layer),
        ],
        out_specs=out_spec,
        out_shape=jax.ShapeDtypeStruct((out_rows, D_MODEL), F32),
        compiler_params=pltpu.CompilerParams(
            dimension_semantics=("arbitrary",), vmem_limit_bytes=VMEM_LIMIT),
    )(a, w_out, r, post_g, mod)


def _pad_lanes_row(v):
    flat = v.astype(F32).reshape(1, 2 * SSD_HEADS)
    return jnp.pad(flat, ((0, 0), (0, SSD_DT_PAD - 2 * SSD_HEADS)))


def _head_expand():
    rows = jnp.arange(SSD_DT_PAD)[:, None]
    head_of_col = jnp.arange(SSD_DI)[None, :] // SSD_HEADDIM
    return jnp.stack([(rows == head_of_col + d * SSD_HEADS) for d in range(2)]).astype(BF16)


def kernel(x, c, ctx, c_ctx, ada_w, ada_b, pre_g, post_g, ssd_w_in, ssd_conv_w, ssd_conv_b,
           ssd_dt_bias, ssd_a_log, ssd_d, ssd_norm_g, ssd_w_out, na_w_in, na_rpb, na_w_out):
    assert x.shape == (1, SEQ, D_MODEL) and ctx.shape == (1, CTX_LEN, D_MODEL)
    r = jnp.concatenate([ctx[0], x[0]], axis=0)
    cvec = jnp.zeros((8, D_MODEL), F32).at[0].set(c[0]).at[1].set(c_ctx)
    mod = _modulation(cvec, ada_w, ada_b)
    expand = _head_expand()

    for i in range(DEPTH):
        j = i // 2
        pg = pre_g[i].reshape(1, D_MODEL)
        qg = post_g[i].reshape(1, D_MODEL)
        if i % 2 == 0:
            w = ssd_w_in[j]
            w_pad = jnp.pad(w, ((0, 0), (0, SSD_DT_PAD - 2 * SSD_HEADS))).astype(BF16)
            wdt_t = jnp.pad(w[:, SSD_DI + SSD_CONV_CH:].T,
                            ((0, SSD_DT_PAD - 2 * SSD_HEADS), (0, 0))).astype(BF16)
            dtb_row = _pad_lanes_row(ssd_dt_bias[j])
            alog_row = _pad_lanes_row(ssd_a_log[j])
            z, xs, bc, dt, dtt = _ssd_in(
                r, pg, mod, i, w_pad, wdt_t, ssd_conv_w[j], ssd_conv_b[j].reshape(1, SSD_CONV_CH),
                dtb_row, dtb_row.reshape(SSD_DT_PAD, 1))
            yf, yb = _ssd_scan(
                xs, bc, dt, dtt, alog_row, alog_row.reshape(SSD_DT_PAD, 1),
                jnp.repeat(ssd_d[j].astype(F32), SSD_HEADDIM).reshape(1, SSD_DI), expand)
            r = _ssd_out(yf, yb, z, ssd_norm_g[j].reshape(1, SSD_DI), ssd_w_out[j].astype(BF16),
                         r, qg, mod, i)
        else:
            qkv, gate = _na_in(r, pg, mod, i, na_w_in[j].astype(BF16))
            a = _na_attention(qkv, gate, _na_bias_table(na_rpb[j]))
            r = _na_out(a, na_w_out[j].astype(BF16), r, qg, mod, i, latent_only=(i == DEPTH - 1))
    return r[None]
```

```python
import functools

import jax
import jax.numpy as jnp
import numpy as np
from jax import lax
from jax.experimental import pallas as pl
from jax.experimental.pallas import tpu as pltpu

D_MODEL = 1024
SEQ = 16384
DEPTH = 4
GRID_W = 64
GRID_H = SEQ // GRID_W
CTX_LEN = 256
ROWS = CTX_LEN + SEQ

SSD_DI = 2048
SSD_HEADDIM = 64
SSD_HEADS = 32
SSD_STATE = 128
SSD_GROUPS = 8
SSD_HEADS_PER_GROUP = SSD_HEADS // SSD_GROUPS
SSD_GROUP_W = SSD_DI // SSD_GROUPS
SSD_GN = SSD_GROUPS * SSD_STATE
SSD_CONV_CH = SSD_DI + 2 * SSD_GN
SSD_IN = SSD_DI + SSD_CONV_CH + 2 * SSD_HEADS
SSD_DT_PAD = 128

NA_HEADS = 16
NA_HEADDIM = 64
NA_DI = 1024
WIN_R = 8
WIN_C = 16

RMS_EPS = 1e-6
NEG_BIG = -1e30
LOG2_E = 1.4426950408889634

LANES = 128
ROW_TILE = 256
SSD_CHUNK = 128
NA_QROWS = 4
NA_QBLK = NA_QROWS * GRID_W
NA_PAIRS_PER_STEP = 2
VMEM_LIMIT = 56 * 1024 * 1024

F32 = jnp.float32
BF16 = jnp.bfloat16


def _silu(x):
    return x / (1.0 + jnp.exp(-x))


def _softplus(x):
    return jnp.maximum(x, 0.0) + jnp.log(1.0 + jnp.exp(-jnp.abs(x)))


def _dot(a, b):
    return jnp.dot(a, b, preferred_element_type=F32)


def _dot_nt(a, b):
    return lax.dot_general(a, b, (((1,), (1,)), ((), ())), preferred_element_type=F32)


def _dot_split2(v, sel):
    hi = v.astype(BF16)
    lo = (v - hi.astype(F32)).astype(BF16)
    return _dot(hi, sel) + _dot(lo, sel)


def _split3(v):
    hi = v.astype(BF16)
    r1 = v - hi.astype(F32)
    mid = r1.astype(BF16)
    lo = (r1 - mid.astype(F32)).astype(BF16)
    return hi, mid, lo


def _dot_exact_rhs(sel, v):
    hi, mid, lo = _split3(v)
    return _dot(sel, hi) + _dot(sel, mid) + _dot(sel, lo)


def _dot_exact_lhs(v, sel):
    hi, mid, lo = _split3(v)
    return _dot(hi, sel) + _dot(mid, sel) + _dot(lo, sel)


def _mod_kernel(c_ref, w_ref, b_ref, o_ref):
    s = _silu(c_ref[...]).astype(BF16)
    o_ref[0] = _dot(s, w_ref[0].astype(BF16)) + b_ref[0]


def _modulation(cvec, ada_w, ada_b):
    tn = 1024
    return pl.pallas_call(
        _mod_kernel,
        name="modulation",
        grid=(DEPTH, 3 * D_MODEL // tn),
        in_specs=[
            pl.BlockSpec((8, D_MODEL), lambda i, j: (0, 0)),
            pl.BlockSpec((1, D_MODEL, tn), lambda i, j: (i, 0, j)),
            pl.BlockSpec((1, 1, tn), lambda i, j: (i, 0, j)),
        ],
        out_specs=pl.BlockSpec((1, 8, tn), lambda i, j: (i, 0, j)),
        out_shape=jax.ShapeDtypeStruct((DEPTH, 8, 3 * D_MODEL), F32),
        compiler_params=pltpu.CompilerParams(vmem_limit_bytes=VMEM_LIMIT),
    )(cvec, ada_w, ada_b.reshape(DEPTH, 1, 3 * D_MODEL))


def _mod_rows(mod_ref, is_ctx):
    m = mod_ref[0]
    row = jnp.where(is_ctx, m[1:2, :], m[0:1, :])
    return row[:, :D_MODEL], row[:, D_MODEL:2 * D_MODEL], row[:, 2 * D_MODEL:]


def _is_ctx_tile():
    return pl.program_id(0) * ROW_TILE < CTX_LEN


def _prenorm(x, g_ref, mod_ref):
    is_ctx = _is_ctx_tile()
    shift, scale, _ = _mod_rows(mod_ref, is_ctx)
    y = x * lax.rsqrt(jnp.mean(x * x, axis=-1, keepdims=True) + RMS_EPS) * g_ref[...]
    return (y * (1.0 + scale) + shift).astype(BF16)


def _postnorm_residual(o, r, g_ref, mod_ref, out_ref):
    _, _, gate = _mod_rows(mod_ref, _is_ctx_tile())
    y = o * lax.rsqrt(jnp.mean(o * o, axis=-1, keepdims=True) + RMS_EPS) * g_ref[...]
    out_ref[...] = r + gate * y


def _row_spec(width):
    return pl.BlockSpec((ROW_TILE, width), lambda i: (i, 0))


def _const_spec(shape):
    nd = len(shape)
    return pl.BlockSpec(shape, lambda i: (0,) * nd)


def _layer_spec(shape, layer):
    nd = len(shape)
    return pl.BlockSpec((1,) + shape, lambda i: (layer,) + (0,) * nd)


_COL_CHUNK = 1024
_HALO = 8
_N_ROW_TILES = ROWS // ROW_TILE
_CTX_ROW_TILES = CTX_LEN // ROW_TILE


def _ssd_in_kernel(x_ref, xp_ref, xn_ref, g_ref, mod_ref, w_ref, wdt_ref, cw_ref, cb_ref,
                   dtb_row_ref, dtb_col_ref, z_ref, xs_ref, bc_ref, dt_ref, dtt_ref):
    i = pl.program_id(0)
    h = _prenorm(x_ref[...], g_ref, mod_ref)
    h_ext = jnp.concatenate([_prenorm(xp_ref[...], g_ref, mod_ref), h,
                             _prenorm(xn_ref[...], g_ref, mod_ref)], axis=0)
    first = (i == 0) | (i == _CTX_ROW_TILES)
    last = (i == _CTX_ROW_TILES - 1) | (i == _N_ROW_TILES - 1)
    rid = lax.broadcasted_iota(jnp.int32, (ROW_TILE, 1), 0)
    keep_up = jnp.logical_not(first & (rid == 0))
    keep_dn = jnp.logical_not(last & (rid == ROW_TILE - 1))
    n_ext = ROW_TILE + 2 * _HALO

    for c in range(0, SSD_DI, _COL_CHUNK):
        z_ref[:, c:c + _COL_CHUNK] = _dot(h, w_ref[:, c:c + _COL_CHUNK])
    for c in range(0, SSD_CONV_CH, _COL_CHUNK):
        cs = slice(c, c + _COL_CHUNK)
        t = _dot(h_ext, w_ref[:, SSD_DI + c:SSD_DI + c + _COL_CHUNK])
        raw = t[_HALO:_HALO + ROW_TILE]
        up = jnp.where(keep_up, pltpu.roll(t, 1, axis=0)[_HALO:_HALO + ROW_TILE], 0.0)
        dn = jnp.where(keep_dn, pltpu.roll(t, n_ext - 1, axis=0)[_HALO:_HALO + ROW_TILE], 0.0)
        act = _silu(cw_ref[0:1, cs] * up + cw_ref[1:2, cs] * raw + cw_ref[2:3, cs] * dn
                    + cb_ref[:, cs])
        if c < SSD_DI:
            xs_ref[:, cs] = act
        else:
            bc_ref[:, c - SSD_DI:c - SSD_DI + _COL_CHUNK] = act.astype(BF16)
    dt_ref[...] = _softplus(_dot(h, w_ref[:, SSD_DI + SSD_CONV_CH:]) + dtb_row_ref[...])
    dtt_ref[...] = _softplus(_dot_nt(wdt_ref[...], h) + dtb_col_ref[...])


def _ssd_in(r, pre_g, mod, layer, w_pad, wdt_t, conv_w, conv_b, dtb_row, dtb_col):
    n_in = SSD_DI + SSD_CONV_CH + SSD_DT_PAD
    per8 = ROW_TILE // _HALO
    last8 = ROWS // _HALO - 1
    return pl.pallas_call(
        _ssd_in_kernel,
        name="ssd_in",
        grid=(_N_ROW_TILES,),
        in_specs=[
            _row_spec(D_MODEL),
            pl.BlockSpec((_HALO, D_MODEL), lambda i: (jnp.maximum(i * per8 - 1, 0), 0)),
            pl.BlockSpec((_HALO, D_MODEL), lambda i: (jnp.minimum((i + 1) * per8, last8), 0)),
            _const_spec((1, D_MODEL)),
            _layer_spec((8, 3 * D_MODEL), layer),
            pl.BlockSpec((D_MODEL, n_in), lambda i: (0, 0), pipeline_mode=pl.Buffered(1)),
            _const_spec((SSD_DT_PAD, D_MODEL)),
            _const_spec((3, SSD_CONV_CH)),
            _const_spec((1, SSD_CONV_CH)),
            _const_spec((1, SSD_DT_PAD)),
            _const_spec((SSD_DT_PAD, 1)),
        ],
        out_specs=[
            _row_spec(SSD_DI),
            _row_spec(SSD_DI),
            _row_spec(2 * SSD_GN),
            _row_spec(SSD_DT_PAD),
            pl.BlockSpec((SSD_DT_PAD, ROW_TILE), lambda i: (0, i)),
        ],
        out_shape=[
            jax.ShapeDtypeStruct((ROWS, SSD_DI), F32),
            jax.ShapeDtypeStruct((ROWS, SSD_DI), F32),
            jax.ShapeDtypeStruct((ROWS, 2 * SSD_GN), BF16),
            jax.ShapeDtypeStruct((ROWS, SSD_DT_PAD), F32),
            jax.ShapeDtypeStruct((SSD_DT_PAD, ROWS), F32),
        ],
        compiler_params=pltpu.CompilerParams(vmem_limit_bytes=VMEM_LIMIT),
    )(r, r, r, pre_g, mod, w_pad, wdt_t, conv_w, conv_b, dtb_row, dtb_col)


_N_CHUNKS = ROWS // SSD_CHUNK
_N_CTX_CHUNKS = CTX_LEN // SSD_CHUNK


def _fwd_chunk(i):
    return i


def _bwd_chunk(i):
    return jnp.where(i < _N_CTX_CHUNKS, _N_CTX_CHUNKS - 1 - i, _N_CHUNKS - 1 + _N_CTX_CHUNKS - i)


def _ssd_direction(d, xs_ref, bc_ref, dt_ref, dtt_ref, alog_row_ref, alog_col_ref,
                   dskip_ref, exp_ref, state_ref, y_ref):
    t = SSD_CHUNK
    xs = xs_ref[...]
    xs_b = xs.astype(BF16)

    dt_head = dtt_ref[...]
    a_tok = dt_ref[...] * (-LOG2_E * jnp.exp(alog_row_ref[...]))
    a_head = dt_head * (-LOG2_E * jnp.exp(alog_col_ref[...]))

    li = lax.broadcasted_iota(jnp.int32, (t, t), 0)
    si = lax.broadcasted_iota(jnp.int32, (t, t), 1)
    lower = si <= li
    upper = si >= li
    mask = lower if d == 0 else upper
    tri = jnp.where(mask, 1.0, 0.0).astype(BF16)
    tri_t = jnp.where(upper if d == 0 else lower, 1.0, 0.0).astype(BF16)

    cum = _dot_exact_rhs(tri, a_tok)
    cum_t = _dot_exact_lhs(a_head, tri_t)
    tot = cum[t - 1:t, :] if d == 0 else cum[0:1, :]

    tot_col = cum_t[:, t - 1:t] if d == 0 else cum_t[:, 0:1]
    w_head = jnp.exp2(tot_col - cum_t) * dt_head
    e_tot = _dot_split2(jnp.broadcast_to(jnp.exp2(tot), (8, LANES)), exp_ref[d])[0:1, :]
    src_t = cum_t - jnp.log2(dt_head)

    lane = lax.broadcasted_iota(jnp.int32, (1, LANES), 1)
    first_head = lane < SSD_HEADDIM

    lhs_y, lhs_s, e_in = [], [], []
    for g in range(SSD_GROUPS):
        b_g = bc_ref[:, g * SSD_STATE:(g + 1) * SSD_STATE]
        c_g = bc_ref[:, SSD_GN + g * SSD_STATE:SSD_GN + (g + 1) * SSD_STATE]
        bt_g = b_g.astype(F32).T
        cb = _dot_nt(c_g, b_g)
        for pair in range(SSD_HEADS_PER_GROUP // 2):
            m, btw, e_col = [], [], []
            for k in range(2):
                idx = d * SSD_HEADS + g * SSD_HEADS_PER_GROUP + 2 * pair + k
                col = jnp.broadcast_to(cum[:, idx:idx + 1], (t, LANES))
                m.append((cb * jnp.exp2(jnp.where(mask, col - src_t[idx:idx + 1, :], NEG_BIG))
                          ).astype(BF16))
                btw.append((bt_g * w_head[idx:idx + 1, :]).astype(BF16))
                e_col.append(col)
            lhs_y.append(jnp.concatenate(m, axis=1))
            lhs_s.append(jnp.concatenate(btw, axis=1))
            e_in.append(jnp.exp2(jnp.where(first_head, e_col[0], e_col[1])))
    for g in range(SSD_GROUPS):
        c_g = bc_ref[:, SSD_GN + g * SSD_STATE:SSD_GN + (g + 1) * SSD_STATE]
        for pair in range(SSD_HEADS_PER_GROUP // 2):
            n = g * (SSD_HEADS_PER_GROUP // 2) + pair
            cs = slice(g * SSD_GROUP_W + pair * LANES, g * SSD_GROUP_W + (pair + 1) * LANES)
            ps = slice(pair * LANES, (pair + 1) * LANES)
            x_pair = xs_b[:, cs]
            zero_x = jnp.zeros_like(x_pair)
            x_stack = jnp.concatenate([jnp.where(first_head, x_pair, zero_x),
                                       jnp.where(first_head, zero_x, x_pair)], axis=0)
            state = state_ref[d, g, :, ps]
            y = _dot(lhs_y[n], x_stack) + e_in[n] * _dot(c_g, state.astype(BF16))
            state_ref[d, g, :, ps] = state * e_tot[:, cs] + _dot(lhs_s[n], x_stack)
            if d == 0:
                y = y + xs[:, cs] * dskip_ref[:, cs]
            y_ref[:, cs] = y


def _ssd_scan_kernel(xs_f, bc_f, dt_f, dtt_f, xs_b, bc_b, dt_b, dtt_b,
                     alog_row_ref, alog_col_ref, dskip_ref, exp_ref, yf_ref, yb_ref, state_ref):
    @pl.when(pl.program_id(0) == 0)
    def _():
        state_ref[...] = jnp.zeros_like(state_ref)

    params = (alog_row_ref, alog_col_ref, dskip_ref, exp_ref, state_ref)
    _ssd_direction(0, xs_f, bc_f, dt_f, dtt_f, *params, yf_ref)
    _ssd_direction(1, xs_b, bc_b, dt_b, dtt_b, *params, yb_ref)


def _ssd_scan(xs, bc, dt, dtt, alog_row, alog_col, dskip, expand):
    t = SSD_CHUNK

    def chunk_specs(order):
        return [
            pl.BlockSpec((t, SSD_DI), lambda i: (order(i), 0)),
            pl.BlockSpec((t, 2 * SSD_GN), lambda i: (order(i), 0)),
            pl.BlockSpec((t, SSD_DT_PAD), lambda i: (order(i), 0)),
            pl.BlockSpec((SSD_DT_PAD, t), lambda i: (0, order(i))),
        ]

    y_shape = jax.ShapeDtypeStruct((ROWS, SSD_DI), F32)
    return pl.pallas_call(
        _ssd_scan_kernel,
        name="ssd_scan",
        grid=(_N_CHUNKS,),
        in_specs=chunk_specs(_fwd_chunk) + chunk_specs(_bwd_chunk) + [
            _const_spec((1, SSD_DT_PAD)),
            _const_spec((SSD_DT_PAD, 1)),
            _const_spec((1, SSD_DI)),
            _const_spec((2, SSD_DT_PAD, SSD_DI)),
        ],
        out_specs=[
            pl.BlockSpec((t, SSD_DI), lambda i: (_fwd_chunk(i), 0)),
            pl.BlockSpec((t, SSD_DI), lambda i: (_bwd_chunk(i), 0)),
        ],
        out_shape=[y_shape, y_shape],
        scratch_shapes=[pltpu.VMEM((2, SSD_GROUPS, SSD_STATE, SSD_GROUP_W), F32)],
        compiler_params=pltpu.CompilerParams(
            dimension_semantics=("arbitrary",), vmem_limit_bytes=VMEM_LIMIT),
    )(xs, bc, dt, dtt, xs, bc, dt, dtt, alog_row, alog_col, dskip, expand)


def _ssd_out_kernel(yf_ref, yb_ref, z_ref, ng_ref, w_ref, r_ref, pg_ref, mod_ref, out_ref):
    y = (yf_ref[...] + yb_ref[...]) * _silu(z_ref[...])
    parts = []
    for g in range(SSD_GROUPS):
        yg = y[:, g * SSD_GROUP_W:(g + 1) * SSD_GROUP_W]
        parts.append(yg * lax.rsqrt(jnp.mean(yg * yg, axis=-1, keepdims=True) + RMS_EPS))
    yn = (jnp.concatenate(parts, axis=1) * ng_ref[...]).astype(BF16)
    _postnorm_residual(_dot(yn, w_ref[...]), r_ref[...], pg_ref, mod_ref, out_ref)


def _ssd_out(yf, yb, z, norm_g, w_out, r, post_g, mod, layer):
    return pl.pallas_call(
        _ssd_out_kernel,
        name="ssd_out",
        grid=(ROWS // ROW_TILE,),
        in_specs=[
            _row_spec(SSD_DI), _row_spec(SSD_DI), _row_spec(SSD_DI),
            _const_spec((1, SSD_DI)),
            _const_spec((SSD_DI, D_MODEL)),
            _row_spec(D_MODEL),
            _const_spec((1, D_MODEL)),
            _layer_spec((8, 3 * D_MODEL), layer),
        ],
        out_specs=_row_spec(D_MODEL),
        out_shape=jax.ShapeDtypeStruct((ROWS, D_MODEL), F32),
        compiler_params=pltpu.CompilerParams(vmem_limit_bytes=VMEM_LIMIT),
    )(yf, yb, z, norm_g, w_out, r, post_g, mod)


def _na_in_kernel(x_ref, g_ref, mod_ref, w_ref, qkv_ref, gate_ref):
    h = _prenorm(x_ref[...], g_ref, mod_ref)
    scale = NA_HEADDIM ** -0.5 * LOG2_E
    qkv_ref[:, :NA_DI] = (_dot(h, w_ref[:, :NA_DI]) * scale).astype(BF16)
    for c in range(NA_DI, 3 * NA_DI, _COL_CHUNK):
        qkv_ref[:, c:c + _COL_CHUNK] = _dot(h, w_ref[:, c:c + _COL_CHUNK]).astype(BF16)
    gate_ref[...] = _dot(h, w_ref[:, 3 * NA_DI:])


def _na_in(r, pre_g, mod, layer, w_in):
    return pl.pallas_call(
        _na_in_kernel,
        name="na_in",
        grid=(ROWS // ROW_TILE,),
        in_specs=[
            _row_spec(D_MODEL),
            _const_spec((1, D_MODEL)),
            _layer_spec((8, 3 * D_MODEL), layer),
            pl.BlockSpec((D_MODEL, 4 * NA_DI), lambda i: (0, 0), pipeline_mode=pl.Buffered(1)),
        ],
        out_specs=[_row_spec(3 * NA_DI), _row_spec(NA_DI)],
        out_shape=[
            jax.ShapeDtypeStruct((ROWS, 3 * NA_DI), BF16),
            jax.ShapeDtypeStruct((ROWS, NA_DI), F32),
        ],
        compiler_params=pltpu.CompilerParams(vmem_limit_bytes=VMEM_LIMIT),
    )(r, pre_g, mod, w_in)


_NA_BLOCKS = ROWS // NA_QBLK
_NA_WIN_ROWS = 3 * NA_QROWS
_NA_WIN_KEYS = WIN_R * GRID_W
_NA_BIAS_VARIANTS = WIN_R
_NA_STEP_W = NA_PAIRS_PER_STEP * LANES


def _na_bias_table(rpb):
    n = GRID_W
    cols = np.arange(n)
    col_start = np.clip(cols - WIN_C // 2, 0, n - WIN_C)
    cvalid = (cols[None, :] >= col_start[:, None]) & (cols[None, :] < col_start[:, None] + WIN_C)

    rpb = rpb.astype(F32)
    edge = n - WIN_C
    ext = jnp.concatenate([jnp.repeat(rpb[..., :1], edge, axis=-1), rpb,
                           jnp.repeat(rpb[..., -1:], edge + 1, axis=-1)], axis=-1)
    skew = jnp.tile(ext, (1, 1, n))[..., :n * (2 * n - 1)].reshape(NA_HEADS, 2 * WIN_R - 1, n, 2 * n - 1)
    toep = jnp.where(cvalid, skew[..., n - 1:] * LOG2_E, NEG_BIG)
    return jnp.stack(
        [jnp.concatenate([toep[:, v + j] for j in range(WIN_R)], axis=-1)
         for v in range(_NA_BIAS_VARIANTS)], axis=1)


def _softmax_stats(scores):
    m = functools.reduce(jnp.maximum, [jnp.max(x, axis=-1, keepdims=True) for x in scores])
    p = [jnp.exp2(x - m) for x in scores]
    inv = 1.0 / functools.reduce(jnp.add, [jnp.sum(x, axis=-1, keepdims=True) for x in p])
    return [x.astype(BF16) for x in p], inv


def _na_kernel(q_ref, kp_ref, kc_ref, kn_ref, kx_ref, vp_ref, vc_ref, vn_ref, vx_ref,
               gate_ref, bias_ref, o_ref, kwin_ref, vwin_ref):
    rb = pl.program_id(1)
    lane = lax.broadcasted_iota(jnp.int32, (1, LANES), 1)
    first_head = lane < NA_HEADDIM
    heads = [(pair, e) for pair in range(NA_PAIRS_PER_STEP) for e in range(2)]
    lss = [slice(pair * LANES, (pair + 1) * LANES) for pair in range(NA_PAIRS_PER_STEP)]

    def head_queries(pair, e):
        q = q_ref[:, lss[pair]]
        return jnp.where(first_head if e == 0 else ~first_head, q, jnp.zeros_like(q))

    def finish(outs):
        for pair in range(NA_PAIRS_PER_STEP):
            o = jnp.where(first_head, outs[2 * pair], outs[2 * pair + 1])
            o_ref[:, lss[pair]] = (o * _silu(gate_ref[:, lss[pair]])).astype(BF16)

    @pl.when(rb == 0)
    def _():
        outs = []
        for pair, e in heads:
            p, inv = _softmax_stats([_dot_nt(head_queries(pair, e), kx_ref[:, lss[pair]])])
            outs.append(_dot(p[0], vx_ref[:, lss[pair]]) * inv)
        finish(outs)

    @pl.when(rb > 0)
    def _():
        for j, (k_ref, v_ref) in enumerate(((kp_ref, vp_ref), (kc_ref, vc_ref), (kn_ref, vn_ref))):
            kwin_ref[j * NA_QBLK:(j + 1) * NA_QBLK, :] = k_ref[...]
            vwin_ref[j * NA_QBLK:(j + 1) * NA_QBLK, :] = v_ref[...]
        r0 = (rb - 1) * NA_QROWS
        wins, variants = [], []
        for qr in range(NA_QROWS):
            r = r0 + qr
            start = jnp.clip(r - WIN_R // 2, 0, GRID_H - WIN_R)
            variants.append(start - r + (WIN_R - 1))
            wins.append(pl.ds(pl.multiple_of((start - (r0 - NA_QROWS)) * GRID_W, GRID_W), _NA_WIN_KEYS))
        qrows = [slice(qr * GRID_W, (qr + 1) * GRID_W) for qr in range(NA_QROWS)]
        scores = []
        for pair, e in heads:
            q = head_queries(pair, e)
            s_win = jnp.concatenate(
                [_dot_nt(q[qrows[qr]], kwin_ref[wins[qr], lss[pair]])
                 + bias_ref[2 * pair + e, variants[qr]] for qr in range(NA_QROWS)], axis=0)
            scores.append([s_win, _dot_nt(q, kx_ref[:, lss[pair]])])
        stats = [_softmax_stats(sc) for sc in scores]
        outs = []
        for (pair, e), (p, inv) in zip(heads, stats):
            o_win = jnp.concatenate(
                [_dot(p[0][qrows[qr]], vwin_ref[wins[qr], lss[pair]]) for qr in range(NA_QROWS)], axis=0)
            outs.append((o_win + _dot(p[1], vx_ref[:, lss[pair]])) * inv)
        finish(outs)


def _na_attention(qkv, gate, bias):
    nq = NA_DI // _NA_STEP_W

    def blk(row_fn, section):
        return pl.BlockSpec((NA_QBLK, _NA_STEP_W), lambda hq, rb: (row_fn(rb), section * nq + hq))

    prev_rb = lambda rb: jnp.maximum(rb - 1, 0)
    next_rb = lambda rb: jnp.minimum(rb + 1, _NA_BLOCKS - 1)
    same_rb = lambda rb: rb
    ctx_rb = lambda rb: 0
    return pl.pallas_call(
        _na_kernel,
        name="na_attention",
        grid=(nq, _NA_BLOCKS),
        in_specs=[
            blk(same_rb, 0),
            blk(prev_rb, 1), blk(same_rb, 1), blk(next_rb, 1), blk(ctx_rb, 1),
            blk(prev_rb, 2), blk(same_rb, 2), blk(next_rb, 2), blk(ctx_rb, 2),
            pl.BlockSpec((NA_QBLK, _NA_STEP_W), lambda hq, rb: (rb, hq)),
            pl.BlockSpec((2 * NA_PAIRS_PER_STEP, _NA_BIAS_VARIANTS, GRID_W, _NA_WIN_KEYS),
                         lambda hq, rb: (hq, 0, 0, 0)),
        ],
        out_specs=pl.BlockSpec((NA_QBLK, _NA_STEP_W), lambda hq, rb: (rb, hq)),
        out_shape=jax.ShapeDtypeStruct((ROWS, NA_DI), BF16),
        scratch_shapes=[pltpu.VMEM((_NA_WIN_ROWS * GRID_W, _NA_STEP_W), BF16),
                        pltpu.VMEM((_NA_WIN_ROWS * GRID_W, _NA_STEP_W), BF16)],
        compiler_params=pltpu.CompilerParams(vmem_limit_bytes=VMEM_LIMIT),
    )(qkv, qkv, qkv, qkv, qkv, qkv, qkv, qkv, qkv, gate, bias)


def _na_out_kernel(a_ref, w_ref, r_ref, pg_ref, mod_ref, out_ref):
    _postnorm_residual(_dot(a_ref[...], w_ref[...]), r_ref[...], pg_ref, mod_ref, out_ref)


def _na_out(a, w_out, r, post_g, mod, layer, latent_only):
    if latent_only:
        n_ctx = CTX_LEN // ROW_TILE
        out_spec = pl.BlockSpec((ROW_TILE, D_MODEL), lambda i: (jnp.maximum(i - n_ctx, 0), 0))
        out_rows = SEQ
    else:
        out_spec, out_rows = _row_spec(D_MODEL), ROWS
    return pl.pallas_call(
        _na_out_kernel,
        name="na_out",
        grid=(ROWS // ROW_TILE,),
        in_specs=[
            _row_spec(NA_DI),
            _const_spec((NA_DI, D_MODEL)),
            _row_spec(D_MODEL),
            _const_spec((1, D_MODEL)),
            _layer_spec((8, 3 * D_MODEL), layer),
        ],
        out_specs=out_spec,
        out_shape=jax.ShapeDtypeStruct((out_rows, D_MODEL), F32),
        compiler_params=pltpu.CompilerParams(
            dimension_semantics=("arbitrary",), vmem_limit_bytes=VMEM_LIMIT),
    )(a, w_out, r, post_g, mod)


def _pad_lanes_row(v):
    flat = v.astype(F32).reshape(1, 2 * SSD_HEADS)
    return jnp.pad(flat, ((0, 0), (0, SSD_DT_PAD - 2 * SSD_HEADS)))


def _head_expand():
    rows = jnp.arange(SSD_DT_PAD)[:, None]
    head_of_col = jnp.arange(SSD_DI)[None, :] // SSD_HEADDIM
    return jnp.stack([(rows == head_of_col + d * SSD_HEADS) for d in range(2)]).astype(BF16)


def kernel(x, c, ctx, c_ctx, ada_w, ada_b, pre_g, post_g, ssd_w_in, ssd_conv_w, ssd_conv_b,
           ssd_dt_bias, ssd_a_log, ssd_d, ssd_norm_g, ssd_w_out, na_w_in, na_rpb, na_w_out):
    assert x.shape == (1, SEQ, D_MODEL) and ctx.shape == (1, CTX_LEN, D_MODEL)
    r = jnp.concatenate([ctx[0], x[0]], axis=0)
    cvec = jnp.zeros((8, D_MODEL), F32).at[0].set(c[0]).at[1].set(c_ctx)
    mod = _modulation(cvec, ada_w, ada_b)
    expand = _head_expand()

    for i in range(DEPTH):
        j = i // 2
        pg = pre_g[i].reshape(1, D_MODEL)
        qg = post_g[i].reshape(1, D_MODEL)
        if i % 2 == 0:
            w = ssd_w_in[j]
            w_pad = jnp.pad(w, ((0, 0), (0, SSD_DT_PAD - 2 * SSD_HEADS))).astype(BF16)
            wdt_t = jnp.pad(w[:, SSD_DI + SSD_CONV_CH:].T,
                            ((0, SSD_DT_PAD - 2 * SSD_HEADS), (0, 0))).astype(BF16)
            dtb_row = _pad_lanes_row(ssd_dt_bias[j])
            alog_row = _pad_lanes_row(ssd_a_log[j])
            z, xs, bc, dt, dtt = _ssd_in(
                r, pg, mod, i, w_pad, wdt_t, ssd_conv_w[j], ssd_conv_b[j].reshape(1, SSD_CONV_CH),
                dtb_row, dtb_row.reshape(SSD_DT_PAD, 1))
            yf, yb = _ssd_scan(
                xs, bc, dt, dtt, alog_row, alog_row.reshape(SSD_DT_PAD, 1),
                jnp.repeat(ssd_d[j].astype(F32), SSD_HEADDIM).reshape(1, SSD_DI), expand)
            r = _ssd_out(yf, yb, z, ssd_norm_g[j].reshape(1, SSD_DI), ssd_w_out[j].astype(BF16),
                         r, qg, mod, i)
        else:
            qkv, gate = _na_in(r, pg, mod, i, na_w_in[j].astype(BF16))
            a = _na_attention(qkv, gate, _na_bias_table(na_rpb[j]))
            r = _na_out(a, na_w_out[j].astype(BF16), r, qg, mod, i, latent_only=(i == DEPTH - 1))
    return r[None]
```

```python
import functools

import jax
import jax.numpy as jnp
import numpy as np
from jax import lax
from jax.experimental import pallas as pl
from jax.experimental.pallas import tpu as pltpu

D_MODEL = 1024
SEQ = 16384
DEPTH = 4
GRID_W = 64
GRID_H = SEQ // GRID_W
CTX_LEN = 256
ROWS = CTX_LEN + SEQ

SSD_DI = 2048
SSD_HEADDIM = 64
SSD_HEADS = 32
SSD_STATE = 128
SSD_GROUPS = 8
SSD_HEADS_PER_GROUP = SSD_HEADS // SSD_GROUPS
SSD_GROUP_W = SSD_DI // SSD_GROUPS
SSD_GN = SSD_GROUPS * SSD_STATE
SSD_CONV_CH = SSD_DI + 2 * SSD_GN
SSD_IN = SSD_DI + SSD_CONV_CH + 2 * SSD_HEADS
SSD_DT_PAD = 128

NA_HEADS = 16
NA_HEADDIM = 64
NA_DI = 1024
WIN_R = 8
WIN_C = 16

RMS_EPS = 1e-6
NEG_BIG = -1e30
LOG2_E = 1.4426950408889634

LANES = 128
ROW_TILE = 256
SSD_CHUNK = 128
NA_QROWS = 4
NA_QBLK = NA_QROWS * GRID_W
NA_PAIRS_PER_STEP = 4
VMEM_LIMIT = 56 * 1024 * 1024

F32 = jnp.float32
BF16 = jnp.bfloat16


def _silu(x):
    return x / (1.0 + jnp.exp(-x))


def _softplus(x):
    return jnp.maximum(x, 0.0) + jnp.log(1.0 + jnp.exp(-jnp.abs(x)))


def _dot(a, b):
    return jnp.dot(a, b, preferred_element_type=F32)


def _dot_nt(a, b):
    return lax.dot_general(a, b, (((1,), (1,)), ((), ())), preferred_element_type=F32)


def _dot_split2(v, sel):
    hi = v.astype(BF16)
    lo = (v - hi.astype(F32)).astype(BF16)
    return _dot(hi, sel) + _dot(lo, sel)


def _split3(v):
    hi = v.astype(BF16)
    r1 = v - hi.astype(F32)
    mid = r1.astype(BF16)
    lo = (r1 - mid.astype(F32)).astype(BF16)
    return hi, mid, lo


def _dot_exact_rhs(sel, v):
    hi, mid, lo = _split3(v)
    return _dot(sel, hi) + _dot(sel, mid) + _dot(sel, lo)


def _dot_exact_lhs(v, sel):
    hi, mid, lo = _split3(v)
    return _dot(hi, sel) + _dot(mid, sel) + _dot(lo, sel)


def _mod_kernel(c_ref, w_ref, b_ref, o_ref):
    s = _silu(c_ref[...]).astype(BF16)
    o_ref[0] = _dot(s, w_ref[0].astype(BF16)) + b_ref[0]


def _modulation(cvec, ada_w, ada_b):
    tn = 1024
    return pl.pallas_call(
        _mod_kernel,
        name="modulation",
        grid=(DEPTH, 3 * D_MODEL // tn),
        in_specs=[
            pl.BlockSpec((8, D_MODEL), lambda i, j: (0, 0)),
            pl.BlockSpec((1, D_MODEL, tn), lambda i, j: (i, 0, j)),
            pl.BlockSpec((1, 1, tn), lambda i, j: (i, 0, j)),
        ],
        out_specs=pl.BlockSpec((1, 8, tn), lambda i, j: (i, 0, j)),
        out_shape=jax.ShapeDtypeStruct((DEPTH, 8, 3 * D_MODEL), F32),
        compiler_params=pltpu.CompilerParams(vmem_limit_bytes=VMEM_LIMIT),
    )(cvec, ada_w, ada_b.reshape(DEPTH, 1, 3 * D_MODEL))


def _mod_rows(mod_ref, is_ctx):
    m = mod_ref[0]
    row = jnp.where(is_ctx, m[1:2, :], m[0:1, :])
    return row[:, :D_MODEL], row[:, D_MODEL:2 * D_MODEL], row[:, 2 * D_MODEL:]


def _is_ctx_tile():
    return pl.program_id(0) * ROW_TILE < CTX_LEN


def _prenorm(x, g_ref, mod_ref):
    is_ctx = _is_ctx_tile()
    shift, scale, _ = _mod_rows(mod_ref, is_ctx)
    y = x * lax.rsqrt(jnp.mean(x * x, axis=-1, keepdims=True) + RMS_EPS) * g_ref[...]
    return (y * (1.0 + scale) + shift).astype(BF16)


def _postnorm_residual(o, r, g_ref, mod_ref, out_ref):
    _, _, gate = _mod_rows(mod_ref, _is_ctx_tile())
    y = o * lax.rsqrt(jnp.mean(o * o, axis=-1, keepdims=True) + RMS_EPS) * g_ref[...]
    out_ref[...] = r + gate * y


def _row_spec(width):
    return pl.BlockSpec((ROW_TILE, width), lambda i: (i, 0))


def _const_spec(shape):
    nd = len(shape)
    return pl.BlockSpec(shape, lambda i: (0,) * nd)


def _layer_spec(shape, layer):
    nd = len(shape)
    return pl.BlockSpec((1,) + shape, lambda i: (layer,) + (0,) * nd)


_COL_CHUNK = 1024
_HALO = 8
_N_ROW_TILES = ROWS // ROW_TILE
_CTX_ROW_TILES = CTX_LEN // ROW_TILE


def _residual_tile(split, refs):
    if not split:
        return refs[0][...], refs[1:]
    return jnp.where(_is_ctx_tile(), refs[0][...], refs[1][...]), refs[2:]


def _residual_specs(split):
    if not split:
        return [_row_spec(D_MODEL)]
    return [pl.BlockSpec((CTX_LEN, D_MODEL), lambda i: (0, 0)),
            pl.BlockSpec((ROW_TILE, D_MODEL), lambda i: (jnp.maximum(i - _CTX_ROW_TILES, 0), 0))]


def _ssd_in_kernel(*refs, split):
    x, refs = _residual_tile(split, refs)
    (xp_ref, xn_ref, g_ref, mod_ref, w_ref, wdtt_ref, cw_ref, cb_ref,
     dtb_row_ref, dtb_col_ref, z_ref, xs_ref, bc_ref, dt_ref, dtt_ref) = refs
    i = pl.program_id(0)
    h = _prenorm(x, g_ref, mod_ref)
    h_ext = jnp.concatenate([_prenorm(xp_ref[...], g_ref, mod_ref), h,
                             _prenorm(xn_ref[...], g_ref, mod_ref)], axis=0)
    first = (i == 0) | (i == _CTX_ROW_TILES)
    last = (i == _CTX_ROW_TILES - 1) | (i == _N_ROW_TILES - 1)
    rid = lax.broadcasted_iota(jnp.int32, (ROW_TILE, 1), 0)
    keep_up = jnp.logical_not(first & (rid == 0))
    keep_dn = jnp.logical_not(last & (rid == ROW_TILE - 1))
    n_ext = ROW_TILE + 2 * _HALO

    for c in range(0, SSD_DI, _COL_CHUNK):
        z_ref[:, c:c + _COL_CHUNK] = _dot(h, w_ref[:, c:c + _COL_CHUNK])
    for c in range(0, SSD_CONV_CH, _COL_CHUNK):
        cs = slice(c, c + _COL_CHUNK)
        t = _dot(h_ext, w_ref[:, SSD_DI + c:SSD_DI + c + _COL_CHUNK])
        raw = t[_HALO:_HALO + ROW_TILE]
        up = jnp.where(keep_up, pltpu.roll(t, 1, axis=0)[_HALO:_HALO + ROW_TILE], 0.0)
        dn = jnp.where(keep_dn, pltpu.roll(t, n_ext - 1, axis=0)[_HALO:_HALO + ROW_TILE], 0.0)
        act = _silu(cw_ref[0:1, cs] * up + cw_ref[1:2, cs] * raw + cw_ref[2:3, cs] * dn
                    + cb_ref[:, cs])
        if c < SSD_DI:
            xs_ref[:, cs] = act
        else:
            bc_ref[:, c - SSD_DI:c - SSD_DI + _COL_CHUNK] = act.astype(BF16)
    dt_ref[...] = _softplus(_dot(h, w_ref[:, SSD_DI + SSD_CONV_CH:]) + dtb_row_ref[...])
    dtt_ref[...] = _softplus(_dot_nt(wdtt_ref[...], h) + dtb_col_ref[...])


def _ssd_in(resid, pre_g, mod, layer, w_pad, w_dt_t, conv_w, conv_b, dtb_row, dtb_col):
    split = isinstance(resid, tuple)
    per8 = ROW_TILE // _HALO
    if split:
        halo_src = resid[1]
        last8 = SEQ // _HALO - 1
        prev8 = lambda i: (jnp.clip((i - _CTX_ROW_TILES) * per8 - 1, 0, last8), 0)
        next8 = lambda i: (jnp.clip((i - _CTX_ROW_TILES + 1) * per8, 0, last8), 0)
        resid_args = list(resid)
    else:
        halo_src = resid
        last8 = ROWS // _HALO - 1
        prev8 = lambda i: (jnp.maximum(i * per8 - 1, 0), 0)
        next8 = lambda i: (jnp.minimum((i + 1) * per8, last8), 0)
        resid_args = [resid]
    return pl.pallas_call(
        functools.partial(_ssd_in_kernel, split=split),
        name="ssd_in",
        grid=(_N_ROW_TILES,),
        in_specs=_residual_specs(split) + [
            pl.BlockSpec((_HALO, D_MODEL), prev8),
            pl.BlockSpec((_HALO, D_MODEL), next8),
            _const_spec((1, D_MODEL)),
            _layer_spec((8, 3 * D_MODEL), layer),
            pl.BlockSpec((D_MODEL, SSD_DI + SSD_CONV_CH + SSD_DT_PAD), lambda i: (0, 0),
                         pipeline_mode=pl.Buffered(1)),
            _const_spec((SSD_DT_PAD, D_MODEL)),
            _const_spec((3, SSD_CONV_CH)),
            _const_spec((1, SSD_CONV_CH)),
            _const_spec((1, SSD_DT_PAD)),
            _const_spec((SSD_DT_PAD, 1)),
        ],
        out_specs=[
            _row_spec(SSD_DI),
            _row_spec(SSD_DI),
            _row_spec(2 * SSD_GN),
            _row_spec(SSD_DT_PAD),
            pl.BlockSpec((SSD_DT_PAD, ROW_TILE), lambda i: (0, i)),
        ],
        out_shape=[
            jax.ShapeDtypeStruct((ROWS, SSD_DI), F32),
            jax.ShapeDtypeStruct((ROWS, SSD_DI), F32),
            jax.ShapeDtypeStruct((ROWS, 2 * SSD_GN), BF16),
            jax.ShapeDtypeStruct((ROWS, SSD_DT_PAD), F32),
            jax.ShapeDtypeStruct((SSD_DT_PAD, ROWS), F32),
        ],
        compiler_params=pltpu.CompilerParams(vmem_limit_bytes=VMEM_LIMIT),
    )(*resid_args, halo_src, halo_src, pre_g, mod, w_pad, w_dt_t,
      conv_w, conv_b, dtb_row, dtb_col)


_N_CHUNKS = ROWS // SSD_CHUNK
_N_CTX_CHUNKS = CTX_LEN // SSD_CHUNK


def _fwd_chunk(i):
    return i


def _bwd_chunk(i):
    return jnp.where(i < _N_CTX_CHUNKS, _N_CTX_CHUNKS - 1 - i, _N_CHUNKS - 1 + _N_CTX_CHUNKS - i)


def _ssd_direction(d, xs_ref, bc_ref, dt_ref, dtt_ref, alog_row_ref, alog_col_ref,
                   dskip_ref, exp_ref, state_ref, y_ref):
    t = SSD_CHUNK
    xs = xs_ref[...]
    xs_b = xs.astype(BF16)

    dt_head = dtt_ref[...]
    a_tok = dt_ref[...] * (-LOG2_E * jnp.exp(alog_row_ref[...]))
    a_head = dt_head * (-LOG2_E * jnp.exp(alog_col_ref[...]))

    li = lax.broadcasted_iota(jnp.int32, (t, t), 0)
    si = lax.broadcasted_iota(jnp.int32, (t, t), 1)
    lower = si <= li
    upper = si >= li
    mask = lower if d == 0 else upper
    tri = jnp.where(mask, 1.0, 0.0).astype(BF16)
    tri_t = jnp.where(upper if d == 0 else lower, 1.0, 0.0).astype(BF16)

    cum = _dot_exact_rhs(tri, a_tok)
    cum_t = _dot_exact_lhs(a_head, tri_t)
    tot = cum[t - 1:t, :] if d == 0 else cum[0:1, :]

    tot_col = cum_t[:, t - 1:t] if d == 0 else cum_t[:, 0:1]
    w_head = jnp.exp2(tot_col - cum_t) * dt_head
    e_tot = _dot_split2(jnp.broadcast_to(jnp.exp2(tot), (8, LANES)), exp_ref[d])[0:1, :]
    src_t = cum_t - jnp.log2(dt_head)

    lane = lax.broadcasted_iota(jnp.int32, (1, LANES), 1)
    first_head = lane < SSD_HEADDIM

    lhs_y, lhs_s, e_in = [], [], []
    for g in range(SSD_GROUPS):
        b_g = bc_ref[:, g * SSD_STATE:(g + 1) * SSD_STATE]
        c_g = bc_ref[:, SSD_GN + g * SSD_STATE:SSD_GN + (g + 1) * SSD_STATE]
        bt_g = b_g.astype(F32).T
        cb = _dot_nt(c_g, b_g)
        for pair in range(SSD_HEADS_PER_GROUP // 2):
            m, btw, e_col = [], [], []
            for k in range(2):
                idx = d * SSD_HEADS + g * SSD_HEADS_PER_GROUP + 2 * pair + k
                col = jnp.broadcast_to(cum[:, idx:idx + 1], (t, LANES))
                m.append((cb * jnp.exp2(jnp.where(mask, col - src_t[idx:idx + 1, :], NEG_BIG))
                          ).astype(BF16))
                btw.append((bt_g * w_head[idx:idx + 1, :]).astype(BF16))
                e_col.append(col)
            lhs_y.append(jnp.concatenate(m, axis=1))
            lhs_s.append(jnp.concatenate(btw, axis=1))
            e_in.append(jnp.exp2(jnp.where(first_head, e_col[0], e_col[1])))
    for g in range(SSD_GROUPS):
        c_g = bc_ref[:, SSD_GN + g * SSD_STATE:SSD_GN + (g + 1) * SSD_STATE]
        for pair in range(SSD_HEADS_PER_GROUP // 2):
            n = g * (SSD_HEADS_PER_GROUP // 2) + pair
            cs = slice(g * SSD_GROUP_W + pair * LANES, g * SSD_GROUP_W + (pair + 1) * LANES)
            ps = slice(pair * LANES, (pair + 1) * LANES)
            x_pair = xs_b[:, cs]
            zero_x = jnp.zeros_like(x_pair)
            x_stack = jnp.concatenate([jnp.where(first_head, x_pair, zero_x),
                                       jnp.where(first_head, zero_x, x_pair)], axis=0)
            state = state_ref[d, g, :, ps]
            y = _dot(lhs_y[n], x_stack) + e_in[n] * _dot(c_g, state.astype(BF16))
            state_ref[d, g, :, ps] = state * e_tot[:, cs] + _dot(lhs_s[n], x_stack)
            if d == 0:
                y = y + xs[:, cs] * dskip_ref[:, cs]
            y_ref[:, cs] = y


def _ssd_scan_kernel(xs_f, bc_f, dt_f, dtt_f, xs_b, bc_b, dt_b, dtt_b,
                     alog_row_ref, alog_col_ref, dskip_ref, exp_ref, yf_ref, yb_ref, state_ref):
    @pl.when(pl.program_id(0) == 0)
    def _():
        state_ref[...] = jnp.zeros_like(state_ref)

    params = (alog_row_ref, alog_col_ref, dskip_ref, exp_ref, state_ref)
    _ssd_direction(0, xs_f, bc_f, dt_f, dtt_f, *params, yf_ref)
    _ssd_direction(1, xs_b, bc_b, dt_b, dtt_b, *params, yb_ref)


def _ssd_scan(xs, bc, dt, dtt, alog_row, alog_col, dskip, expand):
    t = SSD_CHUNK

    def chunk_specs(order):
        return [
            pl.BlockSpec((t, SSD_DI), lambda i: (order(i), 0)),
            pl.BlockSpec((t, 2 * SSD_GN), lambda i: (order(i), 0)),
            pl.BlockSpec((t, SSD_DT_PAD), lambda i: (order(i), 0)),
            pl.BlockSpec((SSD_DT_PAD, t), lambda i: (0, order(i))),
        ]

    y_shape = jax.ShapeDtypeStruct((ROWS, SSD_DI), F32)
    return pl.pallas_call(
        _ssd_scan_kernel,
        name="ssd_scan",
        grid=(_N_CHUNKS,),
        in_specs=chunk_specs(_fwd_chunk) + chunk_specs(_bwd_chunk) + [
            _const_spec((1, SSD_DT_PAD)),
            _const_spec((SSD_DT_PAD, 1)),
            _const_spec((1, SSD_DI)),
            _const_spec((2, SSD_DT_PAD, SSD_DI)),
        ],
        out_specs=[
            pl.BlockSpec((t, SSD_DI), lambda i: (_fwd_chunk(i), 0)),
            pl.BlockSpec((t, SSD_DI), lambda i: (_bwd_chunk(i), 0)),
        ],
        out_shape=[y_shape, y_shape],
        scratch_shapes=[pltpu.VMEM((2, SSD_GROUPS, SSD_STATE, SSD_GROUP_W), F32)],
        compiler_params=pltpu.CompilerParams(
            dimension_semantics=("arbitrary",), vmem_limit_bytes=VMEM_LIMIT),
    )(xs, bc, dt, dtt, xs, bc, dt, dtt, alog_row, alog_col, dskip, expand)


def _ssd_out_kernel(yf_ref, yb_ref, z_ref, ng_ref, w_ref, pg_ref, mod_ref, *refs, split):
    r, (out_ref,) = _residual_tile(split, refs)
    y = (yf_ref[...] + yb_ref[...]) * _silu(z_ref[...])
    parts = []
    for g in range(SSD_GROUPS):
        yg = y[:, g * SSD_GROUP_W:(g + 1) * SSD_GROUP_W]
        parts.append(yg * lax.rsqrt(jnp.mean(yg * yg, axis=-1, keepdims=True) + RMS_EPS))
    yn = (jnp.concatenate(parts, axis=1) * ng_ref[...]).astype(BF16)
    _postnorm_residual(_dot(yn, w_ref[...]), r, pg_ref, mod_ref, out_ref)


def _ssd_out(yf, yb, z, norm_g, w_out, resid, post_g, mod, layer):
    split = isinstance(resid, tuple)
    return pl.pallas_call(
        functools.partial(_ssd_out_kernel, split=split),
        name="ssd_out",
        grid=(ROWS // ROW_TILE,),
        in_specs=[
            _row_spec(SSD_DI), _row_spec(SSD_DI), _row_spec(SSD_DI),
            _const_spec((1, SSD_DI)),
            _const_spec((SSD_DI, D_MODEL)),
            _const_spec((1, D_MODEL)),
            _layer_spec((8, 3 * D_MODEL), layer),
        ] + _residual_specs(split),
        out_specs=_row_spec(D_MODEL),
        out_shape=jax.ShapeDtypeStruct((ROWS, D_MODEL), F32),
        compiler_params=pltpu.CompilerParams(vmem_limit_bytes=VMEM_LIMIT),
    )(yf, yb, z, norm_g, w_out, post_g, mod, *(resid if split else (resid,)))


def _na_in_kernel(x_ref, g_ref, mod_ref, w_ref, qkv_ref, gate_ref):
    h = _prenorm(x_ref[...], g_ref, mod_ref)
    scale = NA_HEADDIM ** -0.5 * LOG2_E
    qkv_ref[:, :NA_DI] = (_dot(h, w_ref[:, :NA_DI]) * scale).astype(BF16)
    for c in range(NA_DI, 3 * NA_DI, _COL_CHUNK):
        qkv_ref[:, c:c + _COL_CHUNK] = _dot(h, w_ref[:, c:c + _COL_CHUNK]).astype(BF16)
    gate_ref[...] = _dot(h, w_ref[:, 3 * NA_DI:])


def _na_in(r, pre_g, mod, layer, w_in):
    return pl.pallas_call(
        _na_in_kernel,
        name="na_in",
        grid=(ROWS // ROW_TILE,),
        in_specs=[
            _row_spec(D_MODEL),
            _const_spec((1, D_MODEL)),
            _layer_spec((8, 3 * D_MODEL), layer),
            pl.BlockSpec((D_MODEL, 4 * NA_DI), lambda i: (0, 0), pipeline_mode=pl.Buffered(1)),
        ],
        out_specs=[_row_spec(3 * NA_DI), _row_spec(NA_DI)],
        out_shape=[
            jax.ShapeDtypeStruct((ROWS, 3 * NA_DI), BF16),
            jax.ShapeDtypeStruct((ROWS, NA_DI), F32),
        ],
        compiler_params=pltpu.CompilerParams(vmem_limit_bytes=VMEM_LIMIT),
    )(r, pre_g, mod, w_in)


_NA_BLOCKS = ROWS // NA_QBLK
_NA_WIN_ROWS = 3 * NA_QROWS
_NA_WIN_KEYS = WIN_R * GRID_W
_NA_BIAS_VARIANTS = WIN_R
_NA_STEP_W = NA_PAIRS_PER_STEP * LANES


def _na_bias_table(rpb):
    n = GRID_W
    cols = np.arange(n)
    col_start = np.clip(cols - WIN_C // 2, 0, n - WIN_C)
    cvalid = (cols[None, :] >= col_start[:, None]) & (cols[None, :] < col_start[:, None] + WIN_C)

    rpb = rpb.astype(F32)
    edge = n - WIN_C
    ext = jnp.concatenate([jnp.repeat(rpb[..., :1], edge, axis=-1), rpb,
                           jnp.repeat(rpb[..., -1:], edge + 1, axis=-1)], axis=-1)
    skew = jnp.tile(ext, (1, 1, n))[..., :n * (2 * n - 1)].reshape(NA_HEADS, 2 * WIN_R - 1, n, 2 * n - 1)
    toep = jnp.where(cvalid, skew[..., n - 1:] * LOG2_E, NEG_BIG)
    rows = toep.transpose(0, 2, 1, 3).reshape(NA_HEADS, n, (2 * WIN_R - 1) * n)
    table = jnp.stack([rows[:, :, v * n:v * n + _NA_WIN_KEYS] for v in range(_NA_BIAS_VARIANTS)],
                      axis=1)
    return table.reshape(NA_HEADS // 2, 2, _NA_BIAS_VARIANTS, n, _NA_WIN_KEYS).transpose(
        0, 2, 1, 3, 4).reshape(NA_HEADS // 2, _NA_BIAS_VARIANTS, 2 * n, _NA_WIN_KEYS)


def _softmax_stats(scores):
    m = functools.reduce(jnp.maximum, [jnp.max(x, axis=-1, keepdims=True) for x in scores])
    p = [jnp.exp2(x - m) for x in scores]
    inv = 1.0 / functools.reduce(jnp.add, [jnp.sum(x, axis=-1, keepdims=True) for x in p])
    return [x.astype(BF16) for x in p], inv


def _na_kernel(q_ref, kp_ref, kc_ref, kn_ref, kx_ref, vp_ref, vc_ref, vn_ref, vx_ref,
               gate_ref, bias_ref, o_ref, kwin_ref, vwin_ref):
    rb = pl.program_id(1)
    lane = lax.broadcasted_iota(jnp.int32, (1, LANES), 1)
    first_head = lane < NA_HEADDIM
    lss = [slice(pair * LANES, (pair + 1) * LANES) for pair in range(NA_PAIRS_PER_STEP)]
    n, n2 = GRID_W, 2 * GRID_W

    def stacked_queries(pair):
        q = q_ref[:, lss[pair]]
        zero = jnp.zeros_like(q)
        q0 = jnp.where(first_head, q, zero)
        q1 = jnp.where(first_head, zero, q)
        parts = []
        for qr in range(NA_QROWS):
            parts += [q0[qr * n:(qr + 1) * n], q1[qr * n:(qr + 1) * n]]
        return jnp.concatenate(parts, axis=0)

    def finish(pair, o):
        ls = lss[pair]
        for qr in range(NA_QROWS):
            rows = slice(qr * n, (qr + 1) * n)
            merged = jnp.where(first_head, o[qr * n2:qr * n2 + n], o[qr * n2 + n:(qr + 1) * n2])
            o_ref[rows, ls] = (merged * _silu(gate_ref[rows, ls])).astype(BF16)

    @pl.when(rb == 0)
    def _():
        for pair in range(NA_PAIRS_PER_STEP):
            p, inv = _softmax_stats([_dot_nt(stacked_queries(pair), kx_ref[:, lss[pair]])])
            finish(pair, _dot(p[0], vx_ref[:, lss[pair]]) * inv)

    @pl.when(rb > 0)
    def _():
        for j, (k_ref, v_ref) in enumerate(((kp_ref, vp_ref), (kc_ref, vc_ref), (kn_ref, vn_ref))):
            kwin_ref[j * NA_QBLK:(j + 1) * NA_QBLK, :] = k_ref[...]
            vwin_ref[j * NA_QBLK:(j + 1) * NA_QBLK, :] = v_ref[...]
        r0 = (rb - 1) * NA_QROWS
        wins, variants = [], []
        for qr in range(NA_QROWS):
            r = r0 + qr
            start = jnp.clip(r - WIN_R // 2, 0, GRID_H - WIN_R)
            variants.append(start - r + (WIN_R - 1))
            wins.append(pl.ds(pl.multiple_of((start - (r0 - NA_QROWS)) * GRID_W, GRID_W), _NA_WIN_KEYS))
        srows = [slice(qr * n2, (qr + 1) * n2) for qr in range(NA_QROWS)]
        scores = []
        for pair in range(NA_PAIRS_PER_STEP):
            q = stacked_queries(pair)
            s_win = jnp.concatenate(
                [_dot_nt(q[srows[qr]], kwin_ref[wins[qr], lss[pair]]) + bias_ref[pair, variants[qr]]
                 for qr in range(NA_QROWS)], axis=0)
            scores.append([s_win, _dot_nt(q, kx_ref[:, lss[pair]])])
        stats = [_softmax_stats(sc) for sc in scores]
        for pair, (p, inv) in enumerate(stats):
            o_win = jnp.concatenate(
                [_dot(p[0][srows[qr]], vwin_ref[wins[qr], lss[pair]]) for qr in range(NA_QROWS)], axis=0)
            finish(pair, (o_win + _dot(p[1], vx_ref[:, lss[pair]])) * inv)


def _na_attention(qkv, gate, bias):
    nq = NA_DI // _NA_STEP_W

    def blk(row_fn, section):
        return pl.BlockSpec((NA_QBLK, _NA_STEP_W), lambda hq, rb: (row_fn(rb), section * nq + hq))

    prev_rb = lambda rb: jnp.maximum(rb - 1, 0)
    next_rb = lambda rb: jnp.minimum(rb + 1, _NA_BLOCKS - 1)
    same_rb = lambda rb: rb
    ctx_rb = lambda rb: 0
    return pl.pallas_call(
        _na_kernel,
        name="na_attention",
        grid=(nq, _NA_BLOCKS),
        in_specs=[
            blk(same_rb, 0),
            blk(prev_rb, 1), blk(same_rb, 1), blk(next_rb, 1), blk(ctx_rb, 1),
            blk(prev_rb, 2), blk(same_rb, 2), blk(next_rb, 2), blk(ctx_rb, 2),
            pl.BlockSpec((NA_QBLK, _NA_STEP_W), lambda hq, rb: (rb, hq)),
            pl.BlockSpec((NA_PAIRS_PER_STEP, _NA_BIAS_VARIANTS, 2 * GRID_W, _NA_WIN_KEYS),
                         lambda hq, rb: (hq, 0, 0, 0)),
        ],
        out_specs=pl.BlockSpec((NA_QBLK, _NA_STEP_W), lambda hq, rb: (rb, hq)),
        out_shape=jax.ShapeDtypeStruct((ROWS, NA_DI), BF16),
        scratch_shapes=[pltpu.VMEM((_NA_WIN_ROWS * GRID_W, _NA_STEP_W), BF16),
                        pltpu.VMEM((_NA_WIN_ROWS * GRID_W, _NA_STEP_W), BF16)],
        compiler_params=pltpu.CompilerParams(vmem_limit_bytes=VMEM_LIMIT),
    )(qkv, qkv, qkv, qkv, qkv, qkv, qkv, qkv, qkv, gate, bias)


def _na_out_kernel(a_ref, w_ref, r_ref, pg_ref, mod_ref, out_ref):
    _postnorm_residual(_dot(a_ref[...], w_ref[...]), r_ref[...], pg_ref, mod_ref, out_ref)


def _na_out(a, w_out, r, post_g, mod, layer, latent_only):
    if latent_only:
        n_ctx = CTX_LEN // ROW_TILE
        out_spec = pl.BlockSpec((ROW_TILE, D_MODEL), lambda i: (jnp.maximum(i - n_ctx, 0), 0))
        out_rows = SEQ
    else:
        out_spec, out_rows = _row_spec(D_MODEL), ROWS
    return pl.pallas_call(
        _na_out_kernel,
        name="na_out",
        grid=(ROWS // ROW_TILE,),
        in_specs=[
            _row_spec(NA_DI),
            _const_spec((NA_DI, D_MODEL)),
            _row_spec(D_MODEL),
            _const_spec((1, D_MODEL)),
            _layer_spec((8, 3 * D_MODEL), layer),
        ],
        out_specs=out_spec,
        out_shape=jax.ShapeDtypeStruct((out_rows, D_MODEL), F32),
        compiler_params=pltpu.CompilerParams(
            dimension_semantics=("arbitrary",), vmem_limit_bytes=VMEM_LIMIT),
    )(a, w_out, r, post_g, mod)


def _pad_lanes_row(v):
    flat = v.astype(F32).reshape(1, 2 * SSD_HEADS)
    return jnp.pad(flat, ((0, 0), (0, SSD_DT_PAD - 2 * SSD_HEADS)))


def _head_expand():
    rows = jnp.arange(SSD_DT_PAD)[:, None]
    head_of_col = jnp.arange(SSD_DI)[None, :] // SSD_HEADDIM
    return jnp.stack([(rows == head_of_col + d * SSD_HEADS) for d in range(2)]).astype(BF16)


def kernel(x, c, ctx, c_ctx, ada_w, ada_b, pre_g, post_g, ssd_w_in, ssd_conv_w, ssd_conv_b,
           ssd_dt_bias, ssd_a_log, ssd_d, ssd_norm_g, ssd_w_out, na_w_in, na_rpb, na_w_out):
    assert x.shape == (1, SEQ, D_MODEL) and ctx.shape == (1, CTX_LEN, D_MODEL)
    r = (ctx[0], x[0])
    cvec = jnp.zeros((8, D_MODEL), F32).at[0].set(c[0]).at[1].set(c_ctx)
    mod = _modulation(cvec, ada_w, ada_b)
    expand = _head_expand()
    dt_pad = SSD_DT_PAD - 2 * SSD_HEADS

    for i in range(DEPTH):
        j = i // 2
        pg = pre_g[i].reshape(1, D_MODEL)
        qg = post_g[i].reshape(1, D_MODEL)
        if i % 2 == 0:
            w_pad = jnp.pad(ssd_w_in[j].astype(BF16), ((0, 0), (0, dt_pad)))
            dtb_row = _pad_lanes_row(ssd_dt_bias[j])
            alog_row = _pad_lanes_row(ssd_a_log[j])
            z, xs, bc, dt, dtt = _ssd_in(
                r, pg, mod, i, w_pad, w_pad[:, SSD_DI + SSD_CONV_CH:].T, ssd_conv_w[j],
                ssd_conv_b[j].reshape(1, SSD_CONV_CH), dtb_row, dtb_row.reshape(SSD_DT_PAD, 1))
            yf, yb = _ssd_scan(
                xs, bc, dt, dtt, alog_row, alog_row.reshape(SSD_DT_PAD, 1),
                jnp.repeat(ssd_d[j].astype(F32), SSD_HEADDIM).reshape(1, SSD_DI), expand)
            r = _ssd_out(yf, yb, z, ssd_norm_g[j].reshape(1, SSD_DI), ssd_w_out[j].astype(BF16),
                         r, qg, mod, i)
        else:
            qkv, gate = _na_in(r, pg, mod, i, na_w_in[j].astype(BF16))
            a = _na_attention(qkv, gate, _na_bias_table(na_rpb[j]))
            r = _na_out(a, na_w_out[j].astype(BF16), r, qg, mod, i, latent_only=(i == DEPTH - 1))
    return r[None]
```

```python
import functools

import jax
import jax.numpy as jnp
import numpy as np
from jax import lax
from jax.experimental import pallas as pl
from jax.experimental.pallas import tpu as pltpu

D_MODEL = 1024
SEQ = 16384
DEPTH = 4
GRID_W = 64
GRID_H = SEQ // GRID_W
CTX_LEN = 256
ROWS = CTX_LEN + SEQ

SSD_DI = 2048
SSD_HEADDIM = 64
SSD_HEADS = 32
SSD_STATE = 128
SSD_GROUPS = 8
SSD_HEADS_PER_GROUP = SSD_HEADS // SSD_GROUPS
SSD_GROUP_W = SSD_DI // SSD_GROUPS
SSD_GN = SSD_GROUPS * SSD_STATE
SSD_CONV_CH = SSD_DI + 2 * SSD_GN
SSD_IN = SSD_DI + SSD_CONV_CH + 2 * SSD_HEADS
SSD_DT_PAD = 128

NA_HEADS = 16
NA_HEADDIM = 64
NA_DI = 1024
WIN_R = 8
WIN_C = 16

RMS_EPS = 1e-6
NEG_BIG = -1e30
LOG2_E = 1.4426950408889634

LANES = 128
ROW_TILE = 256
SSD_CHUNK = 128
NA_QROWS = 4
NA_QBLK = NA_QROWS * GRID_W
NA_PAIRS_PER_STEP = 4
VMEM_LIMIT = 56 * 1024 * 1024

F32 = jnp.float32
BF16 = jnp.bfloat16


def _silu(x):
    return x / (1.0 + jnp.exp(-x))


def _softplus(x):
    return jnp.maximum(x, 0.0) + jnp.log(1.0 + jnp.exp(-jnp.abs(x)))


def _dot(a, b):
    return jnp.dot(a, b, preferred_element_type=F32)


def _dot_nt(a, b):
    return lax.dot_general(a, b, (((1,), (1,)), ((), ())), preferred_element_type=F32)


def _dot_split2(v, sel):
    hi = v.astype(BF16)
    lo = (v - hi.astype(F32)).astype(BF16)
    return _dot(hi, sel) + _dot(lo, sel)


def _split3(v):
    hi = v.astype(BF16)
    r1 = v - hi.astype(F32)
    mid = r1.astype(BF16)
    lo = (r1 - mid.astype(F32)).astype(BF16)
    return hi, mid, lo


def _dot_exact_rhs(sel, v):
    hi, mid, lo = _split3(v)
    return _dot(sel, hi) + _dot(sel, mid) + _dot(sel, lo)


def _dot_exact_lhs(v, sel):
    hi, mid, lo = _split3(v)
    return _dot(hi, sel) + _dot(mid, sel) + _dot(lo, sel)


def _mod_kernel(c_ref, w_ref, b_ref, o_ref):
    s = _silu(c_ref[...]).astype(BF16)
    o_ref[0] = _dot(s, w_ref[0].astype(BF16)) + b_ref[0]


def _modulation(cvec, ada_w, ada_b):
    tn = 1024
    return pl.pallas_call(
        _mod_kernel,
        name="modulation",
        grid=(DEPTH, 3 * D_MODEL // tn),
        in_specs=[
            pl.BlockSpec((8, D_MODEL), lambda i, j: (0, 0)),
            pl.BlockSpec((1, D_MODEL, tn), lambda i, j: (i, 0, j)),
            pl.BlockSpec((1, 1, tn), lambda i, j: (i, 0, j)),
        ],
        out_specs=pl.BlockSpec((1, 8, tn), lambda i, j: (i, 0, j)),
        out_shape=jax.ShapeDtypeStruct((DEPTH, 8, 3 * D_MODEL), F32),
        compiler_params=pltpu.CompilerParams(vmem_limit_bytes=VMEM_LIMIT),
    )(cvec, ada_w, ada_b.reshape(DEPTH, 1, 3 * D_MODEL))


def _mod_rows(mod_ref, is_ctx):
    m = mod_ref[0]
    row = jnp.where(is_ctx, m[1:2, :], m[0:1, :])
    return row[:, :D_MODEL], row[:, D_MODEL:2 * D_MODEL], row[:, 2 * D_MODEL:]


def _is_ctx_tile():
    return pl.program_id(0) * ROW_TILE < CTX_LEN


def _prenorm(x, g_ref, mod_ref):
    is_ctx = _is_ctx_tile()
    shift, scale, _ = _mod_rows(mod_ref, is_ctx)
    y = x * lax.rsqrt(jnp.mean(x * x, axis=-1, keepdims=True) + RMS_EPS) * g_ref[...]
    return (y * (1.0 + scale) + shift).astype(BF16)


def _postnorm_residual(o, r, g_ref, mod_ref, out_ref):
    _, _, gate = _mod_rows(mod_ref, _is_ctx_tile())
    y = o * lax.rsqrt(jnp.mean(o * o, axis=-1, keepdims=True) + RMS_EPS) * g_ref[...]
    out_ref[...] = r + gate * y


def _row_spec(width):
    return pl.BlockSpec((ROW_TILE, width), lambda i: (i, 0))


def _const_spec(shape):
    nd = len(shape)
    return pl.BlockSpec(shape, lambda i: (0,) * nd)


def _layer_spec(shape, layer):
    nd = len(shape)
    return pl.BlockSpec((1,) + shape, lambda i: (layer,) + (0,) * nd)


_COL_CHUNK = 1024
_HALO = 8
_N_ROW_TILES = ROWS // ROW_TILE
_CTX_ROW_TILES = CTX_LEN // ROW_TILE


def _residual_tile(split, refs):
    if not split:
        return refs[0][...], refs[1:]
    return jnp.where(_is_ctx_tile(), refs[0][...], refs[1][...]), refs[2:]


def _residual_specs(split):
    if not split:
        return [_row_spec(D_MODEL)]
    return [pl.BlockSpec((CTX_LEN, D_MODEL), lambda i: (0, 0)),
            pl.BlockSpec((ROW_TILE, D_MODEL), lambda i: (jnp.maximum(i - _CTX_ROW_TILES, 0), 0))]


def _ssd_in_kernel(*refs, split):
    x, refs = _residual_tile(split, refs)
    (xp_ref, xn_ref, g_ref, mod_ref, w_ref, wdtt_ref, cw_ref, cb_ref,
     dtb_row_ref, dtb_col_ref, xs_ref, bc_ref, dt_ref, dtt_ref) = refs
    i = pl.program_id(0)
    h = _prenorm(x, g_ref, mod_ref)
    h_ext = jnp.concatenate([_prenorm(xp_ref[...], g_ref, mod_ref), h,
                             _prenorm(xn_ref[...], g_ref, mod_ref)], axis=0)
    first = (i == 0) | (i == _CTX_ROW_TILES)
    last = (i == _CTX_ROW_TILES - 1) | (i == _N_ROW_TILES - 1)
    rid = lax.broadcasted_iota(jnp.int32, (ROW_TILE, 1), 0)
    keep_up = jnp.logical_not(first & (rid == 0))
    keep_dn = jnp.logical_not(last & (rid == ROW_TILE - 1))
    n_ext = ROW_TILE + 2 * _HALO

    for c in range(0, SSD_CONV_CH, _COL_CHUNK):
        cs = slice(c, c + _COL_CHUNK)
        t = _dot(h_ext, w_ref[:, cs])
        raw = t[_HALO:_HALO + ROW_TILE]
        up = jnp.where(keep_up, pltpu.roll(t, 1, axis=0)[_HALO:_HALO + ROW_TILE], 0.0)
        dn = jnp.where(keep_dn, pltpu.roll(t, n_ext - 1, axis=0)[_HALO:_HALO + ROW_TILE], 0.0)
        act = _silu(cw_ref[0:1, cs] * up + cw_ref[1:2, cs] * raw + cw_ref[2:3, cs] * dn
                    + cb_ref[:, cs])
        if c < SSD_DI:
            xs_ref[:, cs] = act
        else:
            bc_ref[:, c - SSD_DI:c - SSD_DI + _COL_CHUNK] = act.astype(BF16)
    dt_ref[...] = _softplus(_dot(h, w_ref[:, SSD_CONV_CH:]) + dtb_row_ref[...])
    dtt_ref[...] = _softplus(_dot_nt(wdtt_ref[...], h) + dtb_col_ref[...])


def _ssd_in(resid, pre_g, mod, layer, w_xdt, w_dt_t, conv_w, conv_b, dtb_row, dtb_col):
    split = isinstance(resid, tuple)
    per8 = ROW_TILE // _HALO
    if split:
        halo_src = resid[1]
        last8 = SEQ // _HALO - 1
        prev8 = lambda i: (jnp.clip((i - _CTX_ROW_TILES) * per8 - 1, 0, last8), 0)
        next8 = lambda i: (jnp.clip((i - _CTX_ROW_TILES + 1) * per8, 0, last8), 0)
        resid_args = list(resid)
    else:
        halo_src = resid
        last8 = ROWS // _HALO - 1
        prev8 = lambda i: (jnp.maximum(i * per8 - 1, 0), 0)
        next8 = lambda i: (jnp.minimum((i + 1) * per8, last8), 0)
        resid_args = [resid]
    return pl.pallas_call(
        functools.partial(_ssd_in_kernel, split=split),
        name="ssd_in",
        grid=(_N_ROW_TILES,),
        in_specs=_residual_specs(split) + [
            pl.BlockSpec((_HALO, D_MODEL), prev8),
            pl.BlockSpec((_HALO, D_MODEL), next8),
            _const_spec((1, D_MODEL)),
            _layer_spec((8, 3 * D_MODEL), layer),
            pl.BlockSpec((D_MODEL, SSD_CONV_CH + SSD_DT_PAD), lambda i: (0, 0),
                         pipeline_mode=pl.Buffered(1)),
            _const_spec((SSD_DT_PAD, D_MODEL)),
            _const_spec((3, SSD_CONV_CH)),
            _const_spec((1, SSD_CONV_CH)),
            _const_spec((1, SSD_DT_PAD)),
            _const_spec((SSD_DT_PAD, 1)),
        ],
        out_specs=[
            _row_spec(SSD_DI),
            _row_spec(2 * SSD_GN),
            _row_spec(SSD_DT_PAD),
            pl.BlockSpec((SSD_DT_PAD, ROW_TILE), lambda i: (0, i)),
        ],
        out_shape=[
            jax.ShapeDtypeStruct((ROWS, SSD_DI), F32),
            jax.ShapeDtypeStruct((ROWS, 2 * SSD_GN), BF16),
            jax.ShapeDtypeStruct((ROWS, SSD_DT_PAD), F32),
            jax.ShapeDtypeStruct((SSD_DT_PAD, ROWS), F32),
        ],
        compiler_params=pltpu.CompilerParams(vmem_limit_bytes=VMEM_LIMIT),
    )(*resid_args, halo_src, halo_src, pre_g, mod, w_xdt, w_dt_t,
      conv_w, conv_b, dtb_row, dtb_col)


_N_CHUNKS = ROWS // SSD_CHUNK
_N_CTX_CHUNKS = CTX_LEN // SSD_CHUNK


def _fwd_chunk(i):
    return i


def _bwd_chunk(i):
    return jnp.where(i < _N_CTX_CHUNKS, _N_CTX_CHUNKS - 1 - i, _N_CHUNKS - 1 + _N_CTX_CHUNKS - i)


def _ssd_direction(d, xs_ref, bc_ref, dt_ref, dtt_ref, alog_row_ref, alog_col_ref,
                   dskip_ref, exp_ref, state_ref, y_ref):
    t = SSD_CHUNK
    xs = xs_ref[...]
    xs_b = xs.astype(BF16)

    dt_head = dtt_ref[...]
    a_tok = dt_ref[...] * (-LOG2_E * jnp.exp(alog_row_ref[...]))
    a_head = dt_head * (-LOG2_E * jnp.exp(alog_col_ref[...]))

    li = lax.broadcasted_iota(jnp.int32, (t, t), 0)
    si = lax.broadcasted_iota(jnp.int32, (t, t), 1)
    lower = si <= li
    upper = si >= li
    mask = lower if d == 0 else upper
    tri = jnp.where(mask, 1.0, 0.0).astype(BF16)
    tri_t = jnp.where(upper if d == 0 else lower, 1.0, 0.0).astype(BF16)

    cum = _dot_exact_rhs(tri, a_tok)
    cum_t = _dot_exact_lhs(a_head, tri_t)
    tot = cum[t - 1:t, :] if d == 0 else cum[0:1, :]

    tot_col = cum_t[:, t - 1:t] if d == 0 else cum_t[:, 0:1]
    w_head = jnp.exp2(tot_col - cum_t) * dt_head
    e_tot = _dot_split2(jnp.broadcast_to(jnp.exp2(tot), (8, LANES)), exp_ref[d])[0:1, :]
    src_t = cum_t - jnp.log2(dt_head)

    lane = lax.broadcasted_iota(jnp.int32, (1, LANES), 1)
    first_head = lane < SSD_HEADDIM

    lhs_y, lhs_s, e_in = [], [], []
    for g in range(SSD_GROUPS):
        b_g = bc_ref[:, g * SSD_STATE:(g + 1) * SSD_STATE]
        c_g = bc_ref[:, SSD_GN + g * SSD_STATE:SSD_GN + (g + 1) * SSD_STATE]
        bt_g = b_g.astype(F32).T
        cb = _dot_nt(c_g, b_g)
        for pair in range(SSD_HEADS_PER_GROUP // 2):
            m, btw, e_col = [], [], []
            for k in range(2):
                idx = d * SSD_HEADS + g * SSD_HEADS_PER_GROUP + 2 * pair + k
                col = jnp.broadcast_to(cum[:, idx:idx + 1], (t, LANES))
                m.append((cb * jnp.exp2(jnp.where(mask, col - src_t[idx:idx + 1, :], NEG_BIG))
                          ).astype(BF16))
                btw.append((bt_g * w_head[idx:idx + 1, :]).astype(BF16))
                e_col.append(col)
            lhs_y.append(jnp.concatenate(m, axis=1))
            lhs_s.append(jnp.concatenate(btw, axis=1))
            e_in.append(jnp.exp2(jnp.where(first_head, e_col[0], e_col[1])))
    for g in range(SSD_GROUPS):
        c_g = bc_ref[:, SSD_GN + g * SSD_STATE:SSD_GN + (g + 1) * SSD_STATE]
        for pair in range(SSD_HEADS_PER_GROUP // 2):
            n = g * (SSD_HEADS_PER_GROUP // 2) + pair
            cs = slice(g * SSD_GROUP_W + pair * LANES, g * SSD_GROUP_W + (pair + 1) * LANES)
            ps = slice(pair * LANES, (pair + 1) * LANES)
            x_pair = xs_b[:, cs]
            zero_x = jnp.zeros_like(x_pair)
            x_stack = jnp.concatenate([jnp.where(first_head, x_pair, zero_x),
                                       jnp.where(first_head, zero_x, x_pair)], axis=0)
            state = state_ref[d, g, :, ps]
            y = _dot(lhs_y[n], x_stack) + e_in[n] * _dot(c_g, state.astype(BF16))
            state_ref[d, g, :, ps] = state * e_tot[:, cs] + _dot(lhs_s[n], x_stack)
            if d == 0:
                y = y + xs[:, cs] * dskip_ref[:, cs]
            y_ref[:, cs] = y


def _ssd_scan_kernel(xs_f, bc_f, dt_f, dtt_f, xs_b, bc_b, dt_b, dtt_b,
                     alog_row_ref, alog_col_ref, dskip_ref, exp_ref, yf_ref, yb_ref, state_ref):
    @pl.when(pl.program_id(0) == 0)
    def _():
        state_ref[...] = jnp.zeros_like(state_ref)

    params = (alog_row_ref, alog_col_ref, dskip_ref, exp_ref, state_ref)
    _ssd_direction(0, xs_f, bc_f, dt_f, dtt_f, *params, yf_ref)
    _ssd_direction(1, xs_b, bc_b, dt_b, dtt_b, *params, yb_ref)


def _ssd_scan(xs, bc, dt, dtt, alog_row, alog_col, dskip, expand):
    t = SSD_CHUNK

    def chunk_specs(order):
        return [
            pl.BlockSpec((t, SSD_DI), lambda i: (order(i), 0)),
            pl.BlockSpec((t, 2 * SSD_GN), lambda i: (order(i), 0)),
            pl.BlockSpec((t, SSD_DT_PAD), lambda i: (order(i), 0)),
            pl.BlockSpec((SSD_DT_PAD, t), lambda i: (0, order(i))),
        ]

    y_shape = jax.ShapeDtypeStruct((ROWS, SSD_DI), F32)
    return pl.pallas_call(
        _ssd_scan_kernel,
        name="ssd_scan",
        grid=(_N_CHUNKS,),
        in_specs=chunk_specs(_fwd_chunk) + chunk_specs(_bwd_chunk) + [
            _const_spec((1, SSD_DT_PAD)),
            _const_spec((SSD_DT_PAD, 1)),
            _const_spec((1, SSD_DI)),
            _const_spec((2, SSD_DT_PAD, SSD_DI)),
        ],
        out_specs=[
            pl.BlockSpec((t, SSD_DI), lambda i: (_fwd_chunk(i), 0)),
            pl.BlockSpec((t, SSD_DI), lambda i: (_bwd_chunk(i), 0)),
        ],
        out_shape=[y_shape, y_shape],
        scratch_shapes=[pltpu.VMEM((2, SSD_GROUPS, SSD_STATE, SSD_GROUP_W), F32)],
        compiler_params=pltpu.CompilerParams(
            dimension_semantics=("arbitrary",), vmem_limit_bytes=VMEM_LIMIT),
    )(xs, bc, dt, dtt, xs, bc, dt, dtt, alog_row, alog_col, dskip, expand)


def _ssd_out_kernel(yf_ref, yb_ref, wz_ref, ng_ref, w_ref, pre_g_ref, pg_ref, mod_ref, *refs, split):
    r, (out_ref,) = _residual_tile(split, refs)
    h = _prenorm(r, pre_g_ref, mod_ref)
    parts = []
    for g in range(SSD_GROUPS):
        gs = slice(g * SSD_GROUP_W, (g + 1) * SSD_GROUP_W)
        yg = (yf_ref[:, gs] + yb_ref[:, gs]) * _silu(_dot(h, wz_ref[:, gs]))
        parts.append(yg * lax.rsqrt(jnp.mean(yg * yg, axis=-1, keepdims=True) + RMS_EPS))
    yn = (jnp.concatenate(parts, axis=1) * ng_ref[...]).astype(BF16)
    _postnorm_residual(_dot(yn, w_ref[...]), r, pg_ref, mod_ref, out_ref)


def _ssd_out(yf, yb, w_z, norm_g, w_out, resid, pre_g, post_g, mod, layer):
    split = isinstance(resid, tuple)
    return pl.pallas_call(
        functools.partial(_ssd_out_kernel, split=split),
        name="ssd_out",
        grid=(ROWS // ROW_TILE,),
        in_specs=[
            _row_spec(SSD_DI), _row_spec(SSD_DI),
            _const_spec((D_MODEL, SSD_DI)),
            _const_spec((1, SSD_DI)),
            _const_spec((SSD_DI, D_MODEL)),
            _const_spec((1, D_MODEL)),
            _const_spec((1, D_MODEL)),
            _layer_spec((8, 3 * D_MODEL), layer),
        ] + _residual_specs(split),
        out_specs=_row_spec(D_MODEL),
        out_shape=jax.ShapeDtypeStruct((ROWS, D_MODEL), F32),
        compiler_params=pltpu.CompilerParams(vmem_limit_bytes=VMEM_LIMIT),
    )(yf, yb, w_z, norm_g, w_out, pre_g, post_g, mod, *(resid if split else (resid,)))


def _na_in_kernel(x_ref, g_ref, mod_ref, w_ref, qkv_ref, gate_ref):
    h = _prenorm(x_ref[...], g_ref, mod_ref)
    scale = NA_HEADDIM ** -0.5 * LOG2_E
    qkv_ref[:, :NA_DI] = (_dot(h, w_ref[:, :NA_DI]) * scale).astype(BF16)
    for c in range(NA_DI, 3 * NA_DI, _COL_CHUNK):
        qkv_ref[:, c:c + _COL_CHUNK] = _dot(h, w_ref[:, c:c + _COL_CHUNK]).astype(BF16)
    gate_ref[...] = _dot(h, w_ref[:, 3 * NA_DI:])


def _na_in(r, pre_g, mod, layer, w_in):
    return pl.pallas_call(
        _na_in_kernel,
        name="na_in",
        grid=(ROWS // ROW_TILE,),
        in_specs=[
            _row_spec(D_MODEL),
            _const_spec((1, D_MODEL)),
            _layer_spec((8, 3 * D_MODEL), layer),
            pl.BlockSpec((D_MODEL, 4 * NA_DI), lambda i: (0, 0), pipeline_mode=pl.Buffered(1)),
        ],
        out_specs=[_row_spec(3 * NA_DI), _row_spec(NA_DI)],
        out_shape=[
            jax.ShapeDtypeStruct((ROWS, 3 * NA_DI), BF16),
            jax.ShapeDtypeStruct((ROWS, NA_DI), F32),
        ],
        compiler_params=pltpu.CompilerParams(vmem_limit_bytes=VMEM_LIMIT),
    )(r, pre_g, mod, w_in)


_NA_BLOCKS = ROWS // NA_QBLK
_NA_WIN_ROWS = 3 * NA_QROWS
_NA_WIN_KEYS = WIN_R * GRID_W
_NA_BIAS_VARIANTS = WIN_R
_NA_STEP_W = NA_PAIRS_PER_STEP * LANES


def _na_bias_table(rpb):
    n = GRID_W
    cols = np.arange(n)
    col_start = np.clip(cols - WIN_C // 2, 0, n - WIN_C)
    cvalid = (cols[None, :] >= col_start[:, None]) & (cols[None, :] < col_start[:, None] + WIN_C)

    rpb = rpb.astype(F32)
    edge = n - WIN_C
    ext = jnp.concatenate([jnp.repeat(rpb[..., :1], edge, axis=-1), rpb,
                           jnp.repeat(rpb[..., -1:], edge + 1, axis=-1)], axis=-1)
    skew = jnp.tile(ext, (1, 1, n))[..., :n * (2 * n - 1)].reshape(NA_HEADS, 2 * WIN_R - 1, n, 2 * n - 1)
    toep = jnp.where(cvalid, skew[..., n - 1:] * LOG2_E, NEG_BIG)
    rows = toep.transpose(0, 2, 1, 3).reshape(NA_HEADS, n, (2 * WIN_R - 1) * n)
    table = jnp.stack([rows[:, :, v * n:v * n + _NA_WIN_KEYS] for v in range(_NA_BIAS_VARIANTS)],
                      axis=1)
    return table.reshape(NA_HEADS // 2, 2, _NA_BIAS_VARIANTS, n, _NA_WIN_KEYS).transpose(
        0, 2, 1, 3, 4).reshape(NA_HEADS // 2, _NA_BIAS_VARIANTS, 2 * n, _NA_WIN_KEYS)


def _softmax_stats(scores):
    m = functools.reduce(jnp.maximum, [jnp.max(x, axis=-1, keepdims=True) for x in scores])
    p = [jnp.exp2(x - m) for x in scores]
    inv = 1.0 / functools.reduce(jnp.add, [jnp.sum(x, axis=-1, keepdims=True) for x in p])
    return [x.astype(BF16) for x in p], inv


def _na_kernel(q_ref, kp_ref, kc_ref, kn_ref, kx_ref, vp_ref, vc_ref, vn_ref, vx_ref,
               gate_ref, bias_ref, o_ref, kwin_ref, vwin_ref):
    rb = pl.program_id(1)
    lane = lax.broadcasted_iota(jnp.int32, (1, LANES), 1)
    first_head = lane < NA_HEADDIM
    lss = [slice(pair * LANES, (pair + 1) * LANES) for pair in range(NA_PAIRS_PER_STEP)]
    n, n2 = GRID_W, 2 * GRID_W

    def stacked_queries(pair):
        q = q_ref[:, lss[pair]]
        zero = jnp.zeros_like(q)
        q0 = jnp.where(first_head, q, zero)
        q1 = jnp.where(first_head, zero, q)
        parts = []
        for qr in range(NA_QROWS):
            parts += [q0[qr * n:(qr + 1) * n], q1[qr * n:(qr + 1) * n]]
        return jnp.concatenate(parts, axis=0)

    def finish(pair, o):
        ls = lss[pair]
        for qr in range(NA_QROWS):
            rows = slice(qr * n, (qr + 1) * n)
            merged = jnp.where(first_head, o[qr * n2:qr * n2 + n], o[qr * n2 + n:(qr + 1) * n2])
            o_ref[rows, ls] = (merged * _silu(gate_ref[rows, ls])).astype(BF16)

    @pl.when(rb == 0)
    def _():
        for pair in range(NA_PAIRS_PER_STEP):
            p, inv = _softmax_stats([_dot_nt(stacked_queries(pair), kx_ref[:, lss[pair]])])
            finish(pair, _dot(p[0], vx_ref[:, lss[pair]]) * inv)

    @pl.when(rb > 0)
    def _():
        for j, (k_ref, v_ref) in enumerate(((kp_ref, vp_ref), (kc_ref, vc_ref), (kn_ref, vn_ref))):
            kwin_ref[j * NA_QBLK:(j + 1) * NA_QBLK, :] = k_ref[...]
            vwin_ref[j * NA_QBLK:(j + 1) * NA_QBLK, :] = v_ref[...]
        r0 = (rb - 1) * NA_QROWS
        wins, variants = [], []
        for qr in range(NA_QROWS):
            r = r0 + qr
            start = jnp.clip(r - WIN_R // 2, 0, GRID_H - WIN_R)
            variants.append(start - r + (WIN_R - 1))
            wins.append(pl.ds(pl.multiple_of((start - (r0 - NA_QROWS)) * GRID_W, GRID_W), _NA_WIN_KEYS))
        srows = [slice(qr * n2, (qr + 1) * n2) for qr in range(NA_QROWS)]
        scores = []
        for pair in range(NA_PAIRS_PER_STEP):
            q = stacked_queries(pair)
            s_win = jnp.concatenate(
                [_dot_nt(q[srows[qr]], kwin_ref[wins[qr], lss[pair]]) + bias_ref[pair, variants[qr]]
                 for qr in range(NA_QROWS)], axis=0)
            scores.append([s_win, _dot_nt(q, kx_ref[:, lss[pair]])])
        stats = [_softmax_stats(sc) for sc in scores]
        for pair, (p, inv) in enumerate(stats):
            o_win = jnp.concatenate(
                [_dot(p[0][srows[qr]], vwin_ref[wins[qr], lss[pair]]) for qr in range(NA_QROWS)], axis=0)
            finish(pair, (o_win + _dot(p[1], vx_ref[:, lss[pair]])) * inv)


def _na_attention(qkv, gate, bias):
    nq = NA_DI // _NA_STEP_W

    def blk(row_fn, section):
        return pl.BlockSpec((NA_QBLK, _NA_STEP_W), lambda hq, rb: (row_fn(rb), section * nq + hq))

    prev_rb = lambda rb: jnp.maximum(rb - 1, 0)
    next_rb = lambda rb: jnp.minimum(rb + 1, _NA_BLOCKS - 1)
    same_rb = lambda rb: rb
    ctx_rb = lambda rb: 0
    return pl.pallas_call(
        _na_kernel,
        name="na_attention",
        grid=(nq, _NA_BLOCKS),
        in_specs=[
            blk(same_rb, 0),
            blk(prev_rb, 1), blk(same_rb, 1), blk(next_rb, 1), blk(ctx_rb, 1),
            blk(prev_rb, 2), blk(same_rb, 2), blk(next_rb, 2), blk(ctx_rb, 2),
            pl.BlockSpec((NA_QBLK, _NA_STEP_W), lambda hq, rb: (rb, hq)),
            pl.BlockSpec((NA_PAIRS_PER_STEP, _NA_BIAS_VARIANTS, 2 * GRID_W, _NA_WIN_KEYS),
                         lambda hq, rb: (hq, 0, 0, 0)),
        ],
        out_specs=pl.BlockSpec((NA_QBLK, _NA_STEP_W), lambda hq, rb: (rb, hq)),
        out_shape=jax.ShapeDtypeStruct((ROWS, NA_DI), BF16),
        scratch_shapes=[pltpu.VMEM((_NA_WIN_ROWS * GRID_W, _NA_STEP_W), BF16),
                        pltpu.VMEM((_NA_WIN_ROWS * GRID_W, _NA_STEP_W), BF16)],
        compiler_params=pltpu.CompilerParams(vmem_limit_bytes=VMEM_LIMIT),
    )(qkv, qkv, qkv, qkv, qkv, qkv, qkv, qkv, qkv, gate, bias)


def _na_out_kernel(a_ref, w_ref, r_ref, pg_ref, mod_ref, out_ref):
    _postnorm_residual(_dot(a_ref[...], w_ref[...]), r_ref[...], pg_ref, mod_ref, out_ref)


def _na_out(a, w_out, r, post_g, mod, layer, latent_only):
    if latent_only:
        n_ctx = CTX_LEN // ROW_TILE
        out_spec = pl.BlockSpec((ROW_TILE, D_MODEL), lambda i: (jnp.maximum(i - n_ctx, 0), 0))
        out_rows = SEQ
    else:
        out_spec, out_rows = _row_spec(D_MODEL), ROWS
    return pl.pallas_call(
        _na_out_kernel,
        name="na_out",
        grid=(ROWS // ROW_TILE,),
        in_specs=[
            _row_spec(NA_DI),
            _const_spec((NA_DI, D_MODEL)),
            _row_spec(D_MODEL),
            _const_spec((1, D_MODEL)),
            _layer_spec((8, 3 * D_MODEL), layer),
        ],
        out_specs=out_spec,
        out_shape=jax.ShapeDtypeStruct((out_rows, D_MODEL), F32),
        compiler_params=pltpu.CompilerParams(
            dimension_semantics=("arbitrary",), vmem_limit_bytes=VMEM_LIMIT),
    )(a, w_out, r, post_g, mod)


def _pad_lanes_row(v):
    flat = v.astype(F32).reshape(1, 2 * SSD_HEADS)
    return jnp.pad(flat, ((0, 0), (0, SSD_DT_PAD - 2 * SSD_HEADS)))


def _head_expand():
    rows = jnp.arange(SSD_DT_PAD)[:, None]
    head_of_col = jnp.arange(SSD_DI)[None, :] // SSD_HEADDIM
    return jnp.stack([(rows == head_of_col + d * SSD_HEADS) for d in range(2)]).astype(BF16)


def kernel(x, c, ctx, c_ctx, ada_w, ada_b, pre_g, post_g, ssd_w_in, ssd_conv_w, ssd_conv_b,
           ssd_dt_bias, ssd_a_log, ssd_d, ssd_norm_g, ssd_w_out, na_w_in, na_rpb, na_w_out):
    assert x.shape == (1, SEQ, D_MODEL) and ctx.shape == (1, CTX_LEN, D_MODEL)
    r = (ctx[0], x[0])
    cvec = jnp.zeros((8, D_MODEL), F32).at[0].set(c[0]).at[1].set(c_ctx)
    mod = _modulation(cvec, ada_w, ada_b)
    expand = _head_expand()
    dt_pad = SSD_DT_PAD - 2 * SSD_HEADS

    for i in range(DEPTH):
        j = i // 2
        pg = pre_g[i].reshape(1, D_MODEL)
        qg = post_g[i].reshape(1, D_MODEL)
        if i % 2 == 0:
            w_z = ssd_w_in[j, :, :SSD_DI].astype(BF16)
            w_xdt = jnp.pad(ssd_w_in[j, :, SSD_DI:].astype(BF16), ((0, 0), (0, dt_pad)))
            dtb_row = _pad_lanes_row(ssd_dt_bias[j])
            alog_row = _pad_lanes_row(ssd_a_log[j])
            xs, bc, dt, dtt = _ssd_in(
                r, pg, mod, i, w_xdt, w_xdt[:, SSD_CONV_CH:].T, ssd_conv_w[j],
                ssd_conv_b[j].reshape(1, SSD_CONV_CH), dtb_row, dtb_row.reshape(SSD_DT_PAD, 1))
            yf, yb = _ssd_scan(
                xs, bc, dt, dtt, alog_row, alog_row.reshape(SSD_DT_PAD, 1),
                jnp.repeat(ssd_d[j].astype(F32), SSD_HEADDIM).reshape(1, SSD_DI), expand)
            r = _ssd_out(yf, yb, w_z, ssd_norm_g[j].reshape(1, SSD_DI), ssd_w_out[j].astype(BF16),
                         r, pg, qg, mod, i)
        else:
            qkv, gate = _na_in(r, pg, mod, i, na_w_in[j].astype(BF16))
            a = _na_attention(qkv, gate, _na_bias_table(na_rpb[j]))
            r = _na_out(a, na_w_out[j].astype(BF16), r, qg, mod, i, latent_only=(i == DEPTH - 1))
    return r[None]
```

```python
import functools

import jax
import jax.numpy as jnp
import numpy as np
from jax import lax
from jax.experimental import pallas as pl
from jax.experimental.pallas import tpu as pltpu

D_MODEL = 1024
SEQ = 16384
DEPTH = 4
GRID_W = 64
GRID_H = SEQ // GRID_W
CTX_LEN = 256
ROWS = CTX_LEN + SEQ

SSD_DI = 2048
SSD_HEADDIM = 64
SSD_HEADS = 32
SSD_STATE = 128
SSD_GROUPS = 8
SSD_HEADS_PER_GROUP = SSD_HEADS // SSD_GROUPS
SSD_GROUP_W = SSD_DI // SSD_GROUPS
SSD_GN = SSD_GROUPS * SSD_STATE
SSD_CONV_CH = SSD_DI + 2 * SSD_GN
SSD_IN = SSD_DI + SSD_CONV_CH + 2 * SSD_HEADS
SSD_DT_PAD = 128

NA_HEADS = 16
NA_HEADDIM = 64
NA_DI = 1024
WIN_R = 8
WIN_C = 16

RMS_EPS = 1e-6
NEG_BIG = -1e30
LOG2_E = 1.4426950408889634

LANES = 128
ROW_TILE = 256
SSD_CHUNK = 128
NA_QROWS = 4
NA_QBLK = NA_QROWS * GRID_W
NA_PAIRS_PER_STEP = 4
VMEM_LIMIT = 56 * 1024 * 1024

F32 = jnp.float32
BF16 = jnp.bfloat16


def _silu(x):
    return x / (1.0 + jnp.exp(-x))


def _softplus(x):
    return jnp.maximum(x, 0.0) + jnp.log(1.0 + jnp.exp(-jnp.abs(x)))


def _dot(a, b):
    return jnp.dot(a, b, preferred_element_type=F32)


def _dot_nt(a, b):
    return lax.dot_general(a, b, (((1,), (1,)), ((), ())), preferred_element_type=F32)


def _split3(v):
    hi = v.astype(BF16)
    r1 = v - hi.astype(F32)
    mid = r1.astype(BF16)
    lo = (r1 - mid.astype(F32)).astype(BF16)
    return hi, mid, lo


def _dot_exact_rhs(sel, v):
    hi, mid, lo = _split3(v)
    return _dot(sel, hi) + _dot(sel, mid) + _dot(sel, lo)


def _dot_exact_lhs(v, sel):
    hi, mid, lo = _split3(v)
    return _dot(hi, sel) + _dot(mid, sel) + _dot(lo, sel)


def _mod_kernel(c_ref, w_ref, b_ref, o_ref):
    s = _silu(c_ref[...]).astype(BF16)
    o_ref[0] = _dot(s, w_ref[0].astype(BF16)) + b_ref[0]


def _modulation(cvec, ada_w, ada_b):
    tn = 1024
    return pl.pallas_call(
        _mod_kernel,
        name="modulation",
        grid=(DEPTH, 3 * D_MODEL // tn),
        in_specs=[
            pl.BlockSpec((8, D_MODEL), lambda i, j: (0, 0)),
            pl.BlockSpec((1, D_MODEL, tn), lambda i, j: (i, 0, j)),
            pl.BlockSpec((1, 1, tn), lambda i, j: (i, 0, j)),
        ],
        out_specs=pl.BlockSpec((1, 8, tn), lambda i, j: (i, 0, j)),
        out_shape=jax.ShapeDtypeStruct((DEPTH, 8, 3 * D_MODEL), F32),
        compiler_params=pltpu.CompilerParams(vmem_limit_bytes=VMEM_LIMIT),
    )(cvec, ada_w, ada_b.reshape(DEPTH, 1, 3 * D_MODEL))


def _mod_rows(mod_ref, is_ctx):
    m = mod_ref[0]
    row = jnp.where(is_ctx, m[1:2, :], m[0:1, :])
    return row[:, :D_MODEL], row[:, D_MODEL:2 * D_MODEL], row[:, 2 * D_MODEL:]


def _is_ctx_tile():
    return pl.program_id(0) * ROW_TILE < CTX_LEN


def _prenorm(x, g_ref, mod_ref):
    is_ctx = _is_ctx_tile()
    shift, scale, _ = _mod_rows(mod_ref, is_ctx)
    y = x * lax.rsqrt(jnp.mean(x * x, axis=-1, keepdims=True) + RMS_EPS) * g_ref[...]
    return (y * (1.0 + scale) + shift).astype(BF16)


def _postnorm_residual(o, r, g_ref, mod_ref, out_ref):
    _, _, gate = _mod_rows(mod_ref, _is_ctx_tile())
    y = o * lax.rsqrt(jnp.mean(o * o, axis=-1, keepdims=True) + RMS_EPS) * g_ref[...]
    out_ref[...] = r + gate * y


def _row_spec(width):
    return pl.BlockSpec((ROW_TILE, width), lambda i: (i, 0))


def _const_spec(shape):
    nd = len(shape)
    return pl.BlockSpec(shape, lambda i: (0,) * nd)


def _layer_spec(shape, layer, single_buffer=False):
    nd = len(shape)
    mode = dict(pipeline_mode=pl.Buffered(1)) if single_buffer else {}
    return pl.BlockSpec((1,) + shape, lambda i: (layer,) + (0,) * nd, **mode)


_COL_CHUNK = 1024
_HALO = 8
_N_ROW_TILES = ROWS // ROW_TILE
_CTX_ROW_TILES = CTX_LEN // ROW_TILE


def _residual_tile(split, refs):
    if not split:
        return refs[0][...], refs[1:]
    return jnp.where(_is_ctx_tile(), refs[0][...], refs[1][...]), refs[2:]


def _residual_specs(split):
    if not split:
        return [_row_spec(D_MODEL)]
    return [pl.BlockSpec((CTX_LEN, D_MODEL), lambda i: (0, 0)),
            pl.BlockSpec((ROW_TILE, D_MODEL), lambda i: (jnp.maximum(i - _CTX_ROW_TILES, 0), 0))]


def _ssd_in_kernel(*refs, split):
    x, refs = _residual_tile(split, refs)
    (xp_ref, xn_ref, g_ref, mod_ref, w_ref, wdtt_ref, cw_ref, cb_ref,
     dtb_row_ref, dtb_col_ref, xs_ref, bc_ref, dt_ref, dtt_ref) = refs
    i = pl.program_id(0)
    h = _prenorm(x, g_ref, mod_ref)
    h_ext = jnp.concatenate([_prenorm(xp_ref[...], g_ref, mod_ref), h,
                             _prenorm(xn_ref[...], g_ref, mod_ref)], axis=0)
    first = (i == 0) | (i == _CTX_ROW_TILES)
    last = (i == _CTX_ROW_TILES - 1) | (i == _N_ROW_TILES - 1)
    rid = lax.broadcasted_iota(jnp.int32, (ROW_TILE, 1), 0)
    keep_up = jnp.logical_not(first & (rid == 0))
    keep_dn = jnp.logical_not(last & (rid == ROW_TILE - 1))
    n_ext = ROW_TILE + 2 * _HALO

    for c in range(0, SSD_CONV_CH, _COL_CHUNK):
        cs = slice(c, c + _COL_CHUNK)
        t = _dot(h_ext, w_ref[0, :, SSD_DI + c:SSD_DI + c + _COL_CHUNK])
        raw = t[_HALO:_HALO + ROW_TILE]
        up = jnp.where(keep_up, pltpu.roll(t, 1, axis=0)[_HALO:_HALO + ROW_TILE], 0.0)
        dn = jnp.where(keep_dn, pltpu.roll(t, n_ext - 1, axis=0)[_HALO:_HALO + ROW_TILE], 0.0)
        act = _silu(cw_ref[0:1, cs] * up + cw_ref[1:2, cs] * raw + cw_ref[2:3, cs] * dn
                    + cb_ref[:, cs])
        if c < SSD_DI:
            xs_ref[:, cs] = act
        else:
            bc_ref[:, c - SSD_DI:c - SSD_DI + _COL_CHUNK] = act.astype(BF16)
    n_dt = 2 * SSD_HEADS
    dt_ref[:, :n_dt] = _softplus(_dot(h, w_ref[0, :, SSD_DI + SSD_CONV_CH:]) + dtb_row_ref[:, :n_dt])
    dt_ref[:, n_dt:] = jnp.zeros((ROW_TILE, SSD_DT_PAD - n_dt), F32)
    dtt_ref[...] = _softplus(_dot_nt(wdtt_ref[...], h) + dtb_col_ref[...])


def _ssd_in(resid, pre_g, mod, layer, w_in, w_layer, w_dt_t, conv_w, conv_b, dtb_row, dtb_col):
    split = isinstance(resid, tuple)
    per8 = ROW_TILE // _HALO
    if split:
        halo_src = resid[1]
        last8 = SEQ // _HALO - 1
        prev8 = lambda i: (jnp.clip((i - _CTX_ROW_TILES) * per8 - 1, 0, last8), 0)
        next8 = lambda i: (jnp.clip((i - _CTX_ROW_TILES + 1) * per8, 0, last8), 0)
        resid_args = list(resid)
    else:
        halo_src = resid
        last8 = ROWS // _HALO - 1
        prev8 = lambda i: (jnp.maximum(i * per8 - 1, 0), 0)
        next8 = lambda i: (jnp.minimum((i + 1) * per8, last8), 0)
        resid_args = [resid]
    return pl.pallas_call(
        functools.partial(_ssd_in_kernel, split=split),
        name="ssd_in",
        grid=(_N_ROW_TILES,),
        in_specs=_residual_specs(split) + [
            pl.BlockSpec((_HALO, D_MODEL), prev8),
            pl.BlockSpec((_HALO, D_MODEL), next8),
            _const_spec((1, D_MODEL)),
            _layer_spec((8, 3 * D_MODEL), layer),
            _layer_spec((D_MODEL, SSD_IN), w_layer, single_buffer=True),
            _const_spec((SSD_DT_PAD, D_MODEL)),
            _const_spec((3, SSD_CONV_CH)),
            _const_spec((1, SSD_CONV_CH)),
            _const_spec((1, SSD_DT_PAD)),
            _const_spec((SSD_DT_PAD, 1)),
        ],
        out_specs=[
            _row_spec(SSD_DI),
            _row_spec(2 * SSD_GN),
            _row_spec(SSD_DT_PAD),
            pl.BlockSpec((SSD_DT_PAD, ROW_TILE), lambda i: (0, i)),
        ],
        out_shape=[
            jax.ShapeDtypeStruct((ROWS, SSD_DI), F32),
            jax.ShapeDtypeStruct((ROWS, 2 * SSD_GN), BF16),
            jax.ShapeDtypeStruct((ROWS, SSD_DT_PAD), F32),
            jax.ShapeDtypeStruct((SSD_DT_PAD, ROWS), F32),
        ],
        compiler_params=pltpu.CompilerParams(vmem_limit_bytes=VMEM_LIMIT),
    )(*resid_args, halo_src, halo_src, pre_g, mod, w_in, w_dt_t,
      conv_w, conv_b, dtb_row, dtb_col)


_N_CHUNKS = ROWS // SSD_CHUNK
_N_CTX_CHUNKS = CTX_LEN // SSD_CHUNK


def _fwd_chunk(i):
    return i


def _bwd_chunk(i):
    return jnp.where(i < _N_CTX_CHUNKS, _N_CTX_CHUNKS - 1 - i, _N_CHUNKS - 1 + _N_CTX_CHUNKS - i)


def _ssd_direction(d, xs_ref, bc_ref, dt_ref, dtt_ref, alog_row_ref, alog_col_ref,
                   dskip_ref, state_ref, y_ref):
    t = SSD_CHUNK
    xs = xs_ref[...]
    xs_b = xs.astype(BF16)

    dt_head = dtt_ref[...]
    a_tok = dt_ref[...] * (-LOG2_E * jnp.exp(alog_row_ref[...]))
    a_head = dt_head * (-LOG2_E * jnp.exp(alog_col_ref[...]))

    li = lax.broadcasted_iota(jnp.int32, (t, t), 0)
    si = lax.broadcasted_iota(jnp.int32, (t, t), 1)
    lower = si <= li
    upper = si >= li
    mask = lower if d == 0 else upper
    tri = jnp.where(mask, 1.0, 0.0).astype(BF16)
    tri_t = jnp.where(upper if d == 0 else lower, 1.0, 0.0).astype(BF16)

    cum = _dot_exact_rhs(tri, a_tok)
    cum_t = _dot_exact_lhs(a_head, tri_t)
    end = t - 1 if d == 0 else 0

    tot_col = cum_t[:, end:end + 1]
    w_head = jnp.exp2(tot_col - cum_t) * dt_head
    src_t = cum_t - jnp.log2(dt_head)

    lane = lax.broadcasted_iota(jnp.int32, (1, LANES), 1)
    first_head = lane < SSD_HEADDIM

    lhs_y, lhs_s, e_in = [], [], []
    for g in range(SSD_GROUPS):
        b_g = bc_ref[:, g * SSD_STATE:(g + 1) * SSD_STATE]
        c_g = bc_ref[:, SSD_GN + g * SSD_STATE:SSD_GN + (g + 1) * SSD_STATE]
        bt_g = b_g.astype(F32).T
        cb = _dot_nt(c_g, b_g)
        for pair in range(SSD_HEADS_PER_GROUP // 2):
            m, btw, e_col = [], [], []
            for k in range(2):
                idx = d * SSD_HEADS + g * SSD_HEADS_PER_GROUP + 2 * pair + k
                col = jnp.broadcast_to(cum[:, idx:idx + 1], (t, LANES))
                m.append((cb * jnp.exp2(jnp.where(mask, col - src_t[idx:idx + 1, :], NEG_BIG))
                          ).astype(BF16))
                btw.append((bt_g * w_head[idx:idx + 1, :]).astype(BF16))
                e_col.append(col)
            lhs_y.append(jnp.concatenate(m, axis=1))
            lhs_s.append(jnp.concatenate(btw, axis=1))
            e_in.append(jnp.exp2(jnp.where(first_head, e_col[0], e_col[1])))
    for g in range(SSD_GROUPS):
        c_g = bc_ref[:, SSD_GN + g * SSD_STATE:SSD_GN + (g + 1) * SSD_STATE]
        for pair in range(SSD_HEADS_PER_GROUP // 2):
            n = g * (SSD_HEADS_PER_GROUP // 2) + pair
            cs = slice(g * SSD_GROUP_W + pair * LANES, g * SSD_GROUP_W + (pair + 1) * LANES)
            ps = slice(pair * LANES, (pair + 1) * LANES)
            x_pair = xs_b[:, cs]
            zero_x = jnp.zeros_like(x_pair)
            x_stack = jnp.concatenate([jnp.where(first_head, x_pair, zero_x),
                                       jnp.where(first_head, zero_x, x_pair)], axis=0)
            state = state_ref[d, g, :, ps]
            y = _dot(lhs_y[n], x_stack) + e_in[n] * _dot(c_g, state.astype(BF16))
            state_ref[d, g, :, ps] = state * e_in[n][end:end + 1, :] + _dot(lhs_s[n], x_stack)
            if d == 0:
                y = y + xs[:, cs] * dskip_ref[:, cs]
            y_ref[:, cs] = y


def _ssd_scan_kernel(xs_f, bc_f, dt_f, dtt_f, xs_b, bc_b, dt_b, dtt_b,
                     alog_row_ref, alog_col_ref, dskip_ref, yf_ref, yb_ref, state_ref):
    @pl.when(pl.program_id(0) == 0)
    def _():
        state_ref[...] = jnp.zeros_like(state_ref)

    params = (alog_row_ref, alog_col_ref, dskip_ref, state_ref)
    _ssd_direction(0, xs_f, bc_f, dt_f, dtt_f, *params, yf_ref)
    _ssd_direction(1, xs_b, bc_b, dt_b, dtt_b, *params, yb_ref)


def _ssd_scan(xs, bc, dt, dtt, alog_row, alog_col, dskip):
    t = SSD_CHUNK

    def chunk_specs(order):
        return [
            pl.BlockSpec((t, SSD_DI), lambda i: (order(i), 0)),
            pl.BlockSpec((t, 2 * SSD_GN), lambda i: (order(i), 0)),
            pl.BlockSpec((t, SSD_DT_PAD), lambda i: (order(i), 0)),
            pl.BlockSpec((SSD_DT_PAD, t), lambda i: (0, order(i))),
        ]

    y_shape = jax.ShapeDtypeStruct((ROWS, SSD_DI), F32)
    return pl.pallas_call(
        _ssd_scan_kernel,
        name="ssd_scan",
        grid=(_N_CHUNKS,),
        in_specs=chunk_specs(_fwd_chunk) + chunk_specs(_bwd_chunk) + [
            _const_spec((1, SSD_DT_PAD)),
            _const_spec((SSD_DT_PAD, 1)),
            _const_spec((1, SSD_DI)),
        ],
        out_specs=[
            pl.BlockSpec((t, SSD_DI), lambda i: (_fwd_chunk(i), 0)),
            pl.BlockSpec((t, SSD_DI), lambda i: (_bwd_chunk(i), 0)),
        ],
        out_shape=[y_shape, y_shape],
        scratch_shapes=[pltpu.VMEM((2, SSD_GROUPS, SSD_STATE, SSD_GROUP_W), F32)],
        compiler_params=pltpu.CompilerParams(
            dimension_semantics=("arbitrary",), vmem_limit_bytes=VMEM_LIMIT),
    )(xs, bc, dt, dtt, xs, bc, dt, dtt, alog_row, alog_col, dskip)


def _ssd_out_kernel(yf_ref, yb_ref, wz_ref, ng_ref, w_ref, pre_g_ref, pg_ref, mod_ref, *refs, split):
    r, (out_ref,) = _residual_tile(split, refs)
    h = _prenorm(r, pre_g_ref, mod_ref)
    parts = []
    for g in range(SSD_GROUPS):
        gs = slice(g * SSD_GROUP_W, (g + 1) * SSD_GROUP_W)
        yg = (yf_ref[:, gs] + yb_ref[:, gs]) * _silu(_dot(h, wz_ref[0, :, gs]))
        parts.append(yg * lax.rsqrt(jnp.mean(yg * yg, axis=-1, keepdims=True) + RMS_EPS))
    yn = (jnp.concatenate(parts, axis=1) * ng_ref[...]).astype(BF16)
    _postnorm_residual(_dot(yn, w_ref[0]), r, pg_ref, mod_ref, out_ref)


def _ssd_out(yf, yb, w_in, norm_g, w_out, w_layer, resid, pre_g, post_g, mod, layer):
    split = isinstance(resid, tuple)
    return pl.pallas_call(
        functools.partial(_ssd_out_kernel, split=split),
        name="ssd_out",
        grid=(ROWS // ROW_TILE,),
        in_specs=[
            _row_spec(SSD_DI), _row_spec(SSD_DI),
            _layer_spec((D_MODEL, SSD_DI), w_layer),
            _const_spec((1, SSD_DI)),
            _layer_spec((SSD_DI, D_MODEL), w_layer),
            _const_spec((1, D_MODEL)),
            _const_spec((1, D_MODEL)),
            _layer_spec((8, 3 * D_MODEL), layer),
        ] + _residual_specs(split),
        out_specs=_row_spec(D_MODEL),
        out_shape=jax.ShapeDtypeStruct((ROWS, D_MODEL), F32),
        compiler_params=pltpu.CompilerParams(vmem_limit_bytes=VMEM_LIMIT),
    )(yf, yb, w_in, norm_g, w_out, pre_g, post_g, mod, *(resid if split else (resid,)))


def _na_in_kernel(x_ref, g_ref, mod_ref, w_ref, qkv_ref, gate_ref):
    h = _prenorm(x_ref[...], g_ref, mod_ref)
    scale = NA_HEADDIM ** -0.5 * LOG2_E
    qkv_ref[:, :NA_DI] = (_dot(h, w_ref[0, :, :NA_DI]) * scale).astype(BF16)
    for c in range(NA_DI, 3 * NA_DI, _COL_CHUNK):
        qkv_ref[:, c:c + _COL_CHUNK] = _dot(h, w_ref[0, :, c:c + _COL_CHUNK]).astype(BF16)
    gate_ref[...] = _dot(h, w_ref[0, :, 3 * NA_DI:])


def _na_in(r, pre_g, mod, layer, w_in, w_layer):
    return pl.pallas_call(
        _na_in_kernel,
        name="na_in",
        grid=(ROWS // ROW_TILE,),
        in_specs=[
            _row_spec(D_MODEL),
            _const_spec((1, D_MODEL)),
            _layer_spec((8, 3 * D_MODEL), layer),
            _layer_spec((D_MODEL, 4 * NA_DI), w_layer, single_buffer=True),
        ],
        out_specs=[_row_spec(3 * NA_DI), _row_spec(NA_DI)],
        out_shape=[
            jax.ShapeDtypeStruct((ROWS, 3 * NA_DI), BF16),
            jax.ShapeDtypeStruct((ROWS, NA_DI), F32),
        ],
        compiler_params=pltpu.CompilerParams(vmem_limit_bytes=VMEM_LIMIT),
    )(r, pre_g, mod, w_in)


_NA_BLOCKS = ROWS // NA_QBLK
_NA_WIN_ROWS = 3 * NA_QROWS
_NA_WIN_KEYS = WIN_R * GRID_W
_NA_BIAS_VARIANTS = WIN_R
_NA_STEP_W = NA_PAIRS_PER_STEP * LANES


def _na_bias_table(rpb):
    n = GRID_W
    cols = np.arange(n)
    col_start = np.clip(cols - WIN_C // 2, 0, n - WIN_C)
    cvalid = (cols[None, :] >= col_start[:, None]) & (cols[None, :] < col_start[:, None] + WIN_C)

    rpb = rpb.astype(F32)
    edge = n - WIN_C
    ext = jnp.concatenate([jnp.repeat(rpb[..., :1], edge, axis=-1), rpb,
                           jnp.repeat(rpb[..., -1:], edge + 1, axis=-1)], axis=-1)
    skew = jnp.tile(ext, (1, 1, n))[..., :n * (2 * n - 1)].reshape(NA_HEADS, 2 * WIN_R - 1, n, 2 * n - 1)
    toep = jnp.where(cvalid, skew[..., n - 1:] * LOG2_E, NEG_BIG)
    rows = toep.transpose(0, 2, 1, 3).reshape(NA_HEADS, n, (2 * WIN_R - 1) * n)
    table = jnp.stack([rows[:, :, v * n:v * n + _NA_WIN_KEYS] for v in range(_NA_BIAS_VARIANTS)],
                      axis=1)
    return table.reshape(NA_HEADS // 2, 2, _NA_BIAS_VARIANTS, n, _NA_WIN_KEYS).transpose(
        0, 2, 1, 3, 4).reshape(NA_HEADS // 2, _NA_BIAS_VARIANTS, 2 * n, _NA_WIN_KEYS)


def _softmax_stats(scores):
    m = functools.reduce(jnp.maximum, [jnp.max(x, axis=-1, keepdims=True) for x in scores])
    p = [jnp.exp2(x - m) for x in scores]
    inv = 1.0 / functools.reduce(jnp.add, [jnp.sum(x, axis=-1, keepdims=True) for x in p])
    return [x.astype(BF16) for x in p], inv


def _na_kernel(q_ref, kp_ref, kc_ref, kn_ref, kx_ref, vp_ref, vc_ref, vn_ref, vx_ref,
               gate_ref, bias_ref, o_ref, kwin_ref, vwin_ref):
    rb = pl.program_id(1)
    lane = lax.broadcasted_iota(jnp.int32, (1, LANES), 1)
    first_head = lane < NA_HEADDIM
    lss = [slice(pair * LANES, (pair + 1) * LANES) for pair in range(NA_PAIRS_PER_STEP)]
    n, n2 = GRID_W, 2 * GRID_W

    def stacked_queries(pair):
        q = q_ref[:, lss[pair]]
        zero = jnp.zeros_like(q)
        q0 = jnp.where(first_head, q, zero)
        q1 = jnp.where(first_head, zero, q)
        parts = []
        for qr in range(NA_QROWS):
            parts += [q0[qr * n:(qr + 1) * n], q1[qr * n:(qr + 1) * n]]
        return jnp.concatenate(parts, axis=0)

    def finish(pair, o):
        ls = lss[pair]
        for qr in range(NA_QROWS):
            rows = slice(qr * n, (qr + 1) * n)
            merged = jnp.where(first_head, o[qr * n2:qr * n2 + n], o[qr * n2 + n:(qr + 1) * n2])
            o_ref[rows, ls] = (merged * _silu(gate_ref[rows, ls])).astype(BF16)

    @pl.when(rb == 0)
    def _():
        for pair in range(NA_PAIRS_PER_STEP):
            p, inv = _softmax_stats([_dot_nt(stacked_queries(pair), kx_ref[:, lss[pair]])])
            finish(pair, _dot(p[0], vx_ref[:, lss[pair]]) * inv)

    @pl.when(rb > 0)
    def _():
        for j, (k_ref, v_ref) in enumerate(((kp_ref, vp_ref), (kc_ref, vc_ref), (kn_ref, vn_ref))):
            kwin_ref[j * NA_QBLK:(j + 1) * NA_QBLK, :] = k_ref[...]
            vwin_ref[j * NA_QBLK:(j + 1) * NA_QBLK, :] = v_ref[...]
        r0 = (rb - 1) * NA_QROWS
        wins, variants = [], []
        for qr in range(NA_QROWS):
            r = r0 + qr
            start = jnp.clip(r - WIN_R // 2, 0, GRID_H - WIN_R)
            variants.append(start - r + (WIN_R - 1))
            wins.append(pl.ds(pl.multiple_of((start - (r0 - NA_QROWS)) * GRID_W, GRID_W), _NA_WIN_KEYS))
        srows = [slice(qr * n2, (qr + 1) * n2) for qr in range(NA_QROWS)]
        scores = []
        for pair in range(NA_PAIRS_PER_STEP):
            q = stacked_queries(pair)
            s_win = jnp.concatenate(
                [_dot_nt(q[srows[qr]], kwin_ref[wins[qr], lss[pair]]) + bias_ref[pair, variants[qr]]
                 for qr in range(NA_QROWS)], axis=0)
            scores.append([s_win, _dot_nt(q, kx_ref[:, lss[pair]])])
        stats = [_softmax_stats(sc) for sc in scores]
        for pair, (p, inv) in enumerate(stats):
            o_win = jnp.concatenate(
                [_dot(p[0][srows[qr]], vwin_ref[wins[qr], lss[pair]]) for qr in range(NA_QROWS)], axis=0)
            finish(pair, (o_win + _dot(p[1], vx_ref[:, lss[pair]])) * inv)


def _na_attention(qkv, gate, bias):
    nq = NA_DI // _NA_STEP_W

    def blk(row_fn, section):
        return pl.BlockSpec((NA_QBLK, _NA_STEP_W), lambda hq, rb: (row_fn(rb), section * nq + hq))

    prev_rb = lambda rb: jnp.maximum(rb - 1, 0)
    next_rb = lambda rb: jnp.minimum(rb + 1, _NA_BLOCKS - 1)
    same_rb = lambda rb: rb
    ctx_rb = lambda rb: 0
    return pl.pallas_call(
        _na_kernel,
        name="na_attention",
        grid=(nq, _NA_BLOCKS),
        in_specs=[
            blk(same_rb, 0),
            blk(prev_rb, 1), blk(same_rb, 1), blk(next_rb, 1), blk(ctx_rb, 1),
            blk(prev_rb, 2), blk(same_rb, 2), blk(next_rb, 2), blk(ctx_rb, 2),
            pl.BlockSpec((NA_QBLK, _NA_STEP_W), lambda hq, rb: (rb, hq)),
            pl.BlockSpec((NA_PAIRS_PER_STEP, _NA_BIAS_VARIANTS, 2 * GRID_W, _NA_WIN_KEYS),
                         lambda hq, rb: (hq, 0, 0, 0)),
        ],
        out_specs=pl.BlockSpec((NA_QBLK, _NA_STEP_W), lambda hq, rb: (rb, hq)),
        out_shape=jax.ShapeDtypeStruct((ROWS, NA_DI), BF16),
        scratch_shapes=[pltpu.VMEM((_NA_WIN_ROWS * GRID_W, _NA_STEP_W), BF16),
                        pltpu.VMEM((_NA_WIN_ROWS * GRID_W, _NA_STEP_W), BF16)],
        compiler_params=pltpu.CompilerParams(vmem_limit_bytes=VMEM_LIMIT),
    )(qkv, qkv, qkv, qkv, qkv, qkv, qkv, qkv, qkv, gate, bias)


def _na_out_kernel(a_ref, w_ref, r_ref, pg_ref, mod_ref, out_ref):
    _postnorm_residual(_dot(a_ref[...], w_ref[0]), r_ref[...], pg_ref, mod_ref, out_ref)


def _na_out(a, w_out, w_layer, r, post_g, mod, layer, latent_only):
    if latent_only:
        n_ctx = CTX_LEN // ROW_TILE
        out_spec = pl.BlockSpec((ROW_TILE, D_MODEL), lambda i: (jnp.maximum(i - n_ctx, 0), 0))
        out_rows = SEQ
    else:
        out_spec, out_rows = _row_spec(D_MODEL), ROWS
    return pl.pallas_call(
        _na_out_kernel,
        name="na_out",
        grid=(ROWS // ROW_TILE,),
        in_specs=[
            _row_spec(NA_DI),
            _layer_spec((NA_DI, D_MODEL), w_layer),
            _row_spec(D_MODEL),
            _const_spec((1, D_MODEL)),
            _layer_spec((8, 3 * D_MODEL), layer),
        ],
        out_specs=out_spec,
        out_shape=jax.ShapeDtypeStruct((out_rows, D_MODEL), F32),
        compiler_params=pltpu.CompilerParams(
            dimension_semantics=("arbitrary",), vmem_limit_bytes=VMEM_LIMIT),
    )(a, w_out, r, post_g, mod)


def _pad_lanes_row(v):
    flat = v.astype(F32).reshape(1, 2 * SSD_HEADS)
    return jnp.pad(flat, ((0, 0), (0, SSD_DT_PAD - 2 * SSD_HEADS)))


def kernel(x, c, ctx, c_ctx, ada_w, ada_b, pre_g, post_g, ssd_w_in, ssd_conv_w, ssd_conv_b,
           ssd_dt_bias, ssd_a_log, ssd_d, ssd_norm_g, ssd_w_out, na_w_in, na_rpb, na_w_out):
    assert x.shape == (1, SEQ, D_MODEL) and ctx.shape == (1, CTX_LEN, D_MODEL)
    r = (ctx[0], x[0])
    cvec = jnp.zeros((8, D_MODEL), F32).at[0].set(c[0]).at[1].set(c_ctx)
    mod = _modulation(cvec, ada_w, ada_b)
    ssd_w_in_b, ssd_w_out_b = ssd_w_in.astype(BF16), ssd_w_out.astype(BF16)
    na_w_in_b, na_w_out_b = na_w_in.astype(BF16), na_w_out.astype(BF16)
    dt_pad = SSD_DT_PAD - 2 * SSD_HEADS

    for i in range(DEPTH):
        j = i // 2
        pg = pre_g[i].reshape(1, D_MODEL)
        qg = post_g[i].reshape(1, D_MODEL)
        if i % 2 == 0:
            w_dt_t = jnp.pad(ssd_w_in_b[j, :, SSD_DI + SSD_CONV_CH:].T, ((0, dt_pad), (0, 0)))
            dtb_row = _pad_lanes_row(ssd_dt_bias[j])
            alog_row = _pad_lanes_row(ssd_a_log[j])
            xs, bc, dt, dtt = _ssd_in(
                r, pg, mod, i, ssd_w_in_b, j, w_dt_t, ssd_conv_w[j],
                ssd_conv_b[j].reshape(1, SSD_CONV_CH), dtb_row, dtb_row.reshape(SSD_DT_PAD, 1))
            yf, yb = _ssd_scan(
                xs, bc, dt, dtt, alog_row, alog_row.reshape(SSD_DT_PAD, 1),
                jnp.repeat(ssd_d[j].astype(F32), SSD_HEADDIM).reshape(1, SSD_DI))
            r = _ssd_out(yf, yb, ssd_w_in_b, ssd_norm_g[j].reshape(1, SSD_DI), ssd_w_out_b, j,
                         r, pg, qg, mod, i)
        else:
            qkv, gate = _na_in(r, pg, mod, i, na_w_in_b, j)
            a = _na_attention(qkv, gate, _na_bias_table(na_rpb[j]))
            r = _na_out(a, na_w_out_b, j, r, qg, mod, i, latent_only=(i == DEPTH - 1))
    return r[None]
```

```python
import functools

import jax
import jax.numpy as jnp
import numpy as np
from jax import lax
from jax.experimental import pallas as pl
from jax.experimental.pallas import tpu as pltpu

D_MODEL = 1024
SEQ = 16384
DEPTH = 4
GRID_W = 64
GRID_H = SEQ // GRID_W
CTX_LEN = 256
ROWS = CTX_LEN + SEQ

SSD_DI = 2048
SSD_HEADDIM = 64
SSD_HEADS = 32
SSD_STATE = 128
SSD_GROUPS = 8
SSD_HEADS_PER_GROUP = SSD_HEADS // SSD_GROUPS
SSD_GROUP_W = SSD_DI // SSD_GROUPS
SSD_GN = SSD_GROUPS * SSD_STATE
SSD_CONV_CH = SSD_DI + 2 * SSD_GN
SSD_IN = SSD_DI + SSD_CONV_CH + 2 * SSD_HEADS
SSD_DT_PAD = 128

NA_HEADS = 16
NA_HEADDIM = 64
NA_DI = 1024
WIN_R = 8
WIN_C = 16

RMS_EPS = 1e-6
NEG_BIG = -1e30
LOG2_E = 1.4426950408889634

LANES = 128
ROW_TILE = 256
SSD_CHUNK = 128
NA_QROWS = 4
NA_QBLK = NA_QROWS * GRID_W
NA_PAIRS_PER_STEP = 4
VMEM_LIMIT = 56 * 1024 * 1024

F32 = jnp.float32
BF16 = jnp.bfloat16
_COL_CHUNK = 1024


def _silu(x):
    return x / (1.0 + jnp.exp(-x))


def _softplus(x):
    return jnp.maximum(x, 0.0) + jnp.log(1.0 + jnp.exp(-jnp.abs(x)))


def _dot(a, b):
    return jnp.dot(a, b, preferred_element_type=F32)


def _dot_nt(a, b):
    return lax.dot_general(a, b, (((1,), (1,)), ((), ())), preferred_element_type=F32)


def _split3(v):
    hi = v.astype(BF16)
    r1 = v - hi.astype(F32)
    mid = r1.astype(BF16)
    lo = (r1 - mid.astype(F32)).astype(BF16)
    return hi, mid, lo


def _dot_exact_rhs(sel, v):
    hi, mid, lo = _split3(v)
    return _dot(sel, hi) + _dot(sel, mid) + _dot(sel, lo)


def _dot_exact_lhs(v, sel):
    hi, mid, lo = _split3(v)
    return _dot(hi, sel) + _dot(mid, sel) + _dot(lo, sel)


def _mod_kernel(c_ref, w_ref, b_ref, o_ref):
    s = _silu(c_ref[...]).astype(BF16)
    o_ref[0] = _dot(s, w_ref[0].astype(BF16)) + b_ref[0]


def _modulation(cvec, ada_w, ada_b):
    tn = 1024
    return pl.pallas_call(
        _mod_kernel,
        name="modulation",
        grid=(DEPTH, 3 * D_MODEL // tn),
        in_specs=[
            pl.BlockSpec((8, D_MODEL), lambda i, j: (0, 0)),
            pl.BlockSpec((1, D_MODEL, tn), lambda i, j: (i, 0, j)),
            pl.BlockSpec((1, 1, tn), lambda i, j: (i, 0, j)),
        ],
        out_specs=pl.BlockSpec((1, 8, tn), lambda i, j: (i, 0, j)),
        out_shape=jax.ShapeDtypeStruct((DEPTH, 8, 3 * D_MODEL), F32),
        compiler_params=pltpu.CompilerParams(vmem_limit_bytes=VMEM_LIMIT),
    )(cvec, ada_w, ada_b.reshape(DEPTH, 1, 3 * D_MODEL))


def _mod_rows(mod_ref, is_ctx):
    m = mod_ref[0]
    row = jnp.where(is_ctx, m[1:2, :], m[0:1, :])
    return row[:, :D_MODEL], row[:, D_MODEL:2 * D_MODEL], row[:, 2 * D_MODEL:]


def _is_ctx_tile():
    return pl.program_id(0) * ROW_TILE < CTX_LEN


def _prenorm(x, g_ref, mod_ref):
    is_ctx = _is_ctx_tile()
    shift, scale, _ = _mod_rows(mod_ref, is_ctx)
    y = x * lax.rsqrt(jnp.mean(x * x, axis=-1, keepdims=True) + RMS_EPS) * g_ref[...]
    return (y * (1.0 + scale) + shift).astype(BF16)


def _postnorm_residual(o, r, g_ref, mod_ref, out_ref):
    _, _, gate = _mod_rows(mod_ref, _is_ctx_tile())
    y = o * lax.rsqrt(jnp.mean(o * o, axis=-1, keepdims=True) + RMS_EPS) * g_ref[...]
    out_ref[...] = r + gate * y


def _row_spec(width):
    return pl.BlockSpec((ROW_TILE, width), lambda i: (i, 0))


def _const_spec(shape):
    nd = len(shape)
    return pl.BlockSpec(shape, lambda i: (0,) * nd)


def _layer_spec(shape, layer, single_buffer=False):
    nd = len(shape)
    mode = dict(pipeline_mode=pl.Buffered(1)) if single_buffer else {}
    return pl.BlockSpec((1,) + shape, lambda i: (layer,) + (0,) * nd, **mode)


def _cast_weight_once(w_ref, wb_ref):
    @pl.when(pl.program_id(0) == 0)
    def _():
        for c in range(0, w_ref.shape[2], _COL_CHUNK):
            wb_ref[:, c:c + _COL_CHUNK] = w_ref[0, :, c:c + _COL_CHUNK].astype(BF16)


_HALO = 8
_N_ROW_TILES = ROWS // ROW_TILE
_CTX_ROW_TILES = CTX_LEN // ROW_TILE


def _residual_tile(split, refs):
    if not split:
        return refs[0][...], refs[1:]
    return jnp.where(_is_ctx_tile(), refs[0][...], refs[1][...]), refs[2:]


def _residual_specs(split):
    if not split:
        return [_row_spec(D_MODEL)]
    return [pl.BlockSpec((CTX_LEN, D_MODEL), lambda i: (0, 0)),
            pl.BlockSpec((ROW_TILE, D_MODEL), lambda i: (jnp.maximum(i - _CTX_ROW_TILES, 0), 0))]


def _ssd_in_kernel(*refs, split):
    x, refs = _residual_tile(split, refs)
    (xp_ref, xn_ref, g_ref, mod_ref, w_ref, wdtt_ref, cw_ref, cb_ref,
     dtb_row_ref, dtb_col_ref, xs_ref, bc_ref, dt_ref, dtt_ref, wb_ref) = refs
    _cast_weight_once(w_ref, wb_ref)
    i = pl.program_id(0)
    h = _prenorm(x, g_ref, mod_ref)
    h_ext = jnp.concatenate([_prenorm(xp_ref[...], g_ref, mod_ref), h,
                             _prenorm(xn_ref[...], g_ref, mod_ref)], axis=0)
    first = (i == 0) | (i == _CTX_ROW_TILES)
    last = (i == _CTX_ROW_TILES - 1) | (i == _N_ROW_TILES - 1)
    rid = lax.broadcasted_iota(jnp.int32, (ROW_TILE, 1), 0)
    keep_up = jnp.logical_not(first & (rid == 0))
    keep_dn = jnp.logical_not(last & (rid == ROW_TILE - 1))
    n_ext = ROW_TILE + 2 * _HALO

    for c in range(0, SSD_CONV_CH, _COL_CHUNK):
        cs = slice(c, c + _COL_CHUNK)
        t = _dot(h_ext, wb_ref[:, SSD_DI + c:SSD_DI + c + _COL_CHUNK])
        raw = t[_HALO:_HALO + ROW_TILE]
        up = jnp.where(keep_up, pltpu.roll(t, 1, axis=0)[_HALO:_HALO + ROW_TILE], 0.0)
        dn = jnp.where(keep_dn, pltpu.roll(t, n_ext - 1, axis=0)[_HALO:_HALO + ROW_TILE], 0.0)
        act = _silu(cw_ref[0:1, cs] * up + cw_ref[1:2, cs] * raw + cw_ref[2:3, cs] * dn
                    + cb_ref[:, cs])
        if c < SSD_DI:
            xs_ref[:, cs] = act
        else:
            bc_ref[:, c - SSD_DI:c - SSD_DI + _COL_CHUNK] = act.astype(BF16)
    n_dt = 2 * SSD_HEADS
    dt_ref[:, :n_dt] = _softplus(_dot(h, wb_ref[:, SSD_DI + SSD_CONV_CH:]) + dtb_row_ref[:, :n_dt])
    dt_ref[:, n_dt:] = jnp.zeros((ROW_TILE, SSD_DT_PAD - n_dt), F32)
    dtt_ref[...] = _softplus(_dot_nt(wdtt_ref[...], h) + dtb_col_ref[...])


def _ssd_in(resid, pre_g, mod, layer, w_in, w_layer, w_dt_t, conv_w, conv_b, dtb_row, dtb_col):
    split = isinstance(resid, tuple)
    per8 = ROW_TILE // _HALO
    if split:
        halo_src = resid[1]
        last8 = SEQ // _HALO - 1
        prev8 = lambda i: (jnp.clip((i - _CTX_ROW_TILES) * per8 - 1, 0, last8), 0)
        next8 = lambda i: (jnp.clip((i - _CTX_ROW_TILES + 1) * per8, 0, last8), 0)
        resid_args = list(resid)
    else:
        halo_src = resid
        last8 = ROWS // _HALO - 1
        prev8 = lambda i: (jnp.maximum(i * per8 - 1, 0), 0)
        next8 = lambda i: (jnp.minimum((i + 1) * per8, last8), 0)
        resid_args = [resid]
    return pl.pallas_call(
        functools.partial(_ssd_in_kernel, split=split),
        name="ssd_in",
        grid=(_N_ROW_TILES,),
        in_specs=_residual_specs(split) + [
            pl.BlockSpec((_HALO, D_MODEL), prev8),
            pl.BlockSpec((_HALO, D_MODEL), next8),
            _const_spec((1, D_MODEL)),
            _layer_spec((8, 3 * D_MODEL), layer),
            _layer_spec((D_MODEL, SSD_IN), w_layer, single_buffer=True),
            _const_spec((SSD_DT_PAD, D_MODEL)),
            _const_spec((3, SSD_CONV_CH)),
            _const_spec((1, SSD_CONV_CH)),
            _const_spec((1, SSD_DT_PAD)),
            _const_spec((SSD_DT_PAD, 1)),
        ],
        out_specs=[
            _row_spec(SSD_DI),
            _row_spec(2 * SSD_GN),
            _row_spec(SSD_DT_PAD),
            pl.BlockSpec((SSD_DT_PAD, ROW_TILE), lambda i: (0, i)),
        ],
        out_shape=[
            jax.ShapeDtypeStruct((ROWS, SSD_DI), F32),
            jax.ShapeDtypeStruct((ROWS, 2 * SSD_GN), BF16),
            jax.ShapeDtypeStruct((ROWS, SSD_DT_PAD), F32),
            jax.ShapeDtypeStruct((SSD_DT_PAD, ROWS), F32),
        ],
        scratch_shapes=[pltpu.VMEM((D_MODEL, SSD_IN), BF16)],
        compiler_params=pltpu.CompilerParams(
            dimension_semantics=("arbitrary",), vmem_limit_bytes=VMEM_LIMIT),
    )(*resid_args, halo_src, halo_src, pre_g, mod, w_in, w_dt_t,
      conv_w, conv_b, dtb_row, dtb_col)


_N_CHUNKS = ROWS // SSD_CHUNK
_N_CTX_CHUNKS = CTX_LEN // SSD_CHUNK


def _fwd_chunk(i):
    return i


def _bwd_chunk(i):
    return jnp.where(i < _N_CTX_CHUNKS, _N_CTX_CHUNKS - 1 - i, _N_CHUNKS - 1 + _N_CTX_CHUNKS - i)


def _ssd_direction(d, xs_ref, bc_ref, dt_ref, dtt_ref, alog_row_ref, alog_col_ref,
                   dskip_ref, state_ref, y_ref):
    t = SSD_CHUNK
    xs = xs_ref[...]
    xs_b = xs.astype(BF16)

    dt_head = dtt_ref[...]
    a_tok = dt_ref[...] * (-LOG2_E * jnp.exp(alog_row_ref[...]))
    a_head = dt_head * (-LOG2_E * jnp.exp(alog_col_ref[...]))

    li = lax.broadcasted_iota(jnp.int32, (t, t), 0)
    si = lax.broadcasted_iota(jnp.int32, (t, t), 1)
    lower = si <= li
    upper = si >= li
    mask = lower if d == 0 else upper
    tri = jnp.where(mask, 1.0, 0.0).astype(BF16)
    tri_t = jnp.where(upper if d == 0 else lower, 1.0, 0.0).astype(BF16)

    cum = _dot_exact_rhs(tri, a_tok)
    cum_t = _dot_exact_lhs(a_head, tri_t)
    end = t - 1 if d == 0 else 0

    tot_col = cum_t[:, end:end + 1]
    w_head = jnp.exp2(tot_col - cum_t) * dt_head
    src_t = cum_t - jnp.log2(dt_head)

    lane = lax.broadcasted_iota(jnp.int32, (1, LANES), 1)
    first_head = lane < SSD_HEADDIM

    lhs_y, lhs_s, e_in = [], [], []
    for g in range(SSD_GROUPS):
        b_g = bc_ref[:, g * SSD_STATE:(g + 1) * SSD_STATE]
        c_g = bc_ref[:, SSD_GN + g * SSD_STATE:SSD_GN + (g + 1) * SSD_STATE]
        bt_g = b_g.astype(F32).T
        cb = _dot_nt(c_g, b_g)
        for pair in range(SSD_HEADS_PER_GROUP // 2):
            m, btw, e_col = [], [], []
            for k in range(2):
                idx = d * SSD_HEADS + g * SSD_HEADS_PER_GROUP + 2 * pair + k
                col = jnp.broadcast_to(cum[:, idx:idx + 1], (t, LANES))
                m.append((cb * jnp.exp2(jnp.where(mask, col - src_t[idx:idx + 1, :], NEG_BIG))
                          ).astype(BF16))
                btw.append((bt_g * w_head[idx:idx + 1, :]).astype(BF16))
                e_col.append(col)
            lhs_y.append(jnp.concatenate(m, axis=1))
            lhs_s.append(jnp.concatenate(btw, axis=1))
            e_in.append(jnp.exp2(jnp.where(first_head, e_col[0], e_col[1])))
    for g in range(SSD_GROUPS):
        c_g = bc_ref[:, SSD_GN + g * SSD_STATE:SSD_GN + (g + 1) * SSD_STATE]
        for pair in range(SSD_HEADS_PER_GROUP // 2):
            n = g * (SSD_HEADS_PER_GROUP // 2) + pair
            cs = slice(g * SSD_GROUP_W + pair * LANES, g * SSD_GROUP_W + (pair + 1) * LANES)
            ps = slice(pair * LANES, (pair + 1) * LANES)
            x_pair = xs_b[:, cs]
            zero_x = jnp.zeros_like(x_pair)
            x_stack = jnp.concatenate([jnp.where(first_head, x_pair, zero_x),
                                       jnp.where(first_head, zero_x, x_pair)], axis=0)
            state = state_ref[d, g, :, ps]
            y = _dot(lhs_y[n], x_stack) + e_in[n] * _dot(c_g, state.astype(BF16))
            state_ref[d, g, :, ps] = state * e_in[n][end:end + 1, :] + _dot(lhs_s[n], x_stack)
            if d == 0:
                y = y + xs[:, cs] * dskip_ref[:, cs]
            y_ref[:, cs] = y


def _ssd_scan_kernel(xs_f, bc_f, dt_f, dtt_f, xs_b, bc_b, dt_b, dtt_b,
                     alog_row_ref, alog_col_ref, dskip_ref, yf_ref, yb_ref, state_ref):
    @pl.when(pl.program_id(0) == 0)
    def _():
        state_ref[...] = jnp.zeros_like(state_ref)

    params = (alog_row_ref, alog_col_ref, dskip_ref, state_ref)
    _ssd_direction(0, xs_f, bc_f, dt_f, dtt_f, *params, yf_ref)
    _ssd_direction(1, xs_b, bc_b, dt_b, dtt_b, *params, yb_ref)


def _ssd_scan(xs, bc, dt, dtt, alog_row, alog_col, dskip):
    t = SSD_CHUNK

    def chunk_specs(order):
        return [
            pl.BlockSpec((t, SSD_DI), lambda i: (order(i), 0)),
            pl.BlockSpec((t, 2 * SSD_GN), lambda i: (order(i), 0)),
            pl.BlockSpec((t, SSD_DT_PAD), lambda i: (order(i), 0)),
            pl.BlockSpec((SSD_DT_PAD, t), lambda i: (0, order(i))),
        ]

    y_shape = jax.ShapeDtypeStruct((ROWS, SSD_DI), F32)
    return pl.pallas_call(
        _ssd_scan_kernel,
        name="ssd_scan",
        grid=(_N_CHUNKS,),
        in_specs=chunk_specs(_fwd_chunk) + chunk_specs(_bwd_chunk) + [
            _const_spec((1, SSD_DT_PAD)),
            _const_spec((SSD_DT_PAD, 1)),
            _const_spec((1, SSD_DI)),
        ],
        out_specs=[
            pl.BlockSpec((t, SSD_DI), lambda i: (_fwd_chunk(i), 0)),
            pl.BlockSpec((t, SSD_DI), lambda i: (_bwd_chunk(i), 0)),
        ],
        out_shape=[y_shape, y_shape],
        scratch_shapes=[pltpu.VMEM((2, SSD_GROUPS, SSD_STATE, SSD_GROUP_W), F32)],
        compiler_params=pltpu.CompilerParams(
            dimension_semantics=("arbitrary",), vmem_limit_bytes=VMEM_LIMIT),
    )(xs, bc, dt, dtt, xs, bc, dt, dtt, alog_row, alog_col, dskip)


def _ssd_out_kernel(yf_ref, yb_ref, wz_ref, ng_ref, w_ref, pre_g_ref, pg_ref, mod_ref, *refs, split):
    r, (out_ref,) = _residual_tile(split, refs)
    h = _prenorm(r, pre_g_ref, mod_ref)
    parts = []
    for g in range(SSD_GROUPS):
        gs = slice(g * SSD_GROUP_W, (g + 1) * SSD_GROUP_W)
        yg = (yf_ref[:, gs] + yb_ref[:, gs]) * _silu(_dot(h, wz_ref[0, :, gs]))
        parts.append(yg * lax.rsqrt(jnp.mean(yg * yg, axis=-1, keepdims=True) + RMS_EPS))
    yn = (jnp.concatenate(parts, axis=1) * ng_ref[...]).astype(BF16)
    _postnorm_residual(_dot(yn, w_ref[0]), r, pg_ref, mod_ref, out_ref)


def _ssd_out(yf, yb, w_in, norm_g, w_out, w_layer, resid, pre_g, post_g, mod, layer):
    split = isinstance(resid, tuple)
    return pl.pallas_call(
        functools.partial(_ssd_out_kernel, split=split),
        name="ssd_out",
        grid=(ROWS // ROW_TILE,),
        in_specs=[
            _row_spec(SSD_DI), _row_spec(SSD_DI),
            _layer_spec((D_MODEL, SSD_DI), w_layer),
            _const_spec((1, SSD_DI)),
            _layer_spec((SSD_DI, D_MODEL), w_layer),
            _const_spec((1, D_MODEL)),
            _const_spec((1, D_MODEL)),
            _layer_spec((8, 3 * D_MODEL), layer),
        ] + _residual_specs(split),
        out_specs=_row_spec(D_MODEL),
        out_shape=jax.ShapeDtypeStruct((ROWS, D_MODEL), F32),
        compiler_params=pltpu.CompilerParams(vmem_limit_bytes=VMEM_LIMIT),
    )(yf, yb, w_in, norm_g, w_out, pre_g, post_g, mod, *(resid if split else (resid,)))


def _na_in_kernel(x_ref, g_ref, mod_ref, w_ref, qkv_ref, gate_ref, wb_ref):
    _cast_weight_once(w_ref, wb_ref)
    h = _prenorm(x_ref[...], g_ref, mod_ref)
    scale = NA_HEADDIM ** -0.5 * LOG2_E
    qkv_ref[:, :NA_DI] = (_dot(h, wb_ref[:, :NA_DI]) * scale).astype(BF16)
    for c in range(NA_DI, 3 * NA_DI, _COL_CHUNK):
        qkv_ref[:, c:c + _COL_CHUNK] = _dot(h, wb_ref[:, c:c + _COL_CHUNK]).astype(BF16)
    gate_ref[...] = _dot(h, wb_ref[:, 3 * NA_DI:])


def _na_in(r, pre_g, mod, layer, w_in, w_layer):
    return pl.pallas_call(
        _na_in_kernel,
        name="na_in",
        grid=(ROWS // ROW_TILE,),
        in_specs=[
            _row_spec(D_MODEL),
            _const_spec((1, D_MODEL)),
            _layer_spec((8, 3 * D_MODEL), layer),
            _layer_spec((D_MODEL, 4 * NA_DI), w_layer, single_buffer=True),
        ],
        out_specs=[_row_spec(3 * NA_DI), _row_spec(NA_DI)],
        out_shape=[
            jax.ShapeDtypeStruct((ROWS, 3 * NA_DI), BF16),
            jax.ShapeDtypeStruct((ROWS, NA_DI), F32),
        ],
        scratch_shapes=[pltpu.VMEM((D_MODEL, 4 * NA_DI), BF16)],
        compiler_params=pltpu.CompilerParams(
            dimension_semantics=("arbitrary",), vmem_limit_bytes=VMEM_LIMIT),
    )(r, pre_g, mod, w_in)


_NA_BLOCKS = ROWS // NA_QBLK
_NA_WIN_ROWS = 3 * NA_QROWS
_NA_WIN_KEYS = WIN_R * GRID_W
_NA_BIAS_VARIANTS = WIN_R
_NA_STEP_W = NA_PAIRS_PER_STEP * LANES


def _na_bias_table(rpb):
    n = GRID_W
    cols = np.arange(n)
    col_start = np.clip(cols - WIN_C // 2, 0, n - WIN_C)
    cvalid = (cols[None, :] >= col_start[:, None]) & (cols[None, :] < col_start[:, None] + WIN_C)
    coff = np.clip(cols[None, :] - cols[:, None], -(WIN_C - 1), WIN_C - 1) + (WIN_C - 1)
    select = (coff[None] == np.arange(2 * WIN_C - 1)[:, None, None]) & cvalid[None]
    toep = jnp.einsum("lhdj,jqk->lhdqk", rpb.astype(F32), jnp.asarray(select * LOG2_E, F32),
                      precision=lax.Precision.HIGHEST)
    toep = toep + jnp.asarray(np.where(cvalid, 0.0, NEG_BIG), F32)
    n_layers = rpb.shape[0]
    rows = toep.transpose(0, 1, 3, 2, 4).reshape(n_layers, NA_HEADS, n, (2 * WIN_R - 1) * n)
    table = jnp.stack([rows[..., v * n:v * n + _NA_WIN_KEYS] for v in range(_NA_BIAS_VARIANTS)],
                      axis=2)
    return table.reshape(n_layers, NA_HEADS // 2, 2, _NA_BIAS_VARIANTS, n, _NA_WIN_KEYS).transpose(
        0, 1, 3, 2, 4, 5).reshape(n_layers, NA_HEADS // 2, _NA_BIAS_VARIANTS, 2 * n, _NA_WIN_KEYS)


def _softmax_stats(scores):
    m = functools.reduce(jnp.maximum, [jnp.max(x, axis=-1, keepdims=True) for x in scores])
    p = [jnp.exp2(x - m) for x in scores]
    inv = 1.0 / functools.reduce(jnp.add, [jnp.sum(x, axis=-1, keepdims=True) for x in p])
    return [x.astype(BF16) for x in p], inv


def _na_kernel(q_ref, kp_ref, kc_ref, kn_ref, kx_ref, vp_ref, vc_ref, vn_ref, vx_ref,
               gate_ref, bias_ref, o_ref, kwin_ref, vwin_ref):
    rb = pl.program_id(1)
    lane = lax.broadcasted_iota(jnp.int32, (1, LANES), 1)
    first_head = lane < NA_HEADDIM
    lss = [slice(pair * LANES, (pair + 1) * LANES) for pair in range(NA_PAIRS_PER_STEP)]
    n, n2 = GRID_W, 2 * GRID_W

    def stacked_queries(pair):
        q = q_ref[:, lss[pair]]
        zero = jnp.zeros_like(q)
        q0 = jnp.where(first_head, q, zero)
        q1 = jnp.where(first_head, zero, q)
        parts = []
        for qr in range(NA_QROWS):
            parts += [q0[qr * n:(qr + 1) * n], q1[qr * n:(qr + 1) * n]]
        return jnp.concatenate(parts, axis=0)

    def finish(pair, o):
        ls = lss[pair]
        for qr in range(NA_QROWS):
            rows = slice(qr * n, (qr + 1) * n)
            merged = jnp.where(first_head, o[qr * n2:qr * n2 + n], o[qr * n2 + n:(qr + 1) * n2])
            o_ref[rows, ls] = (merged * _silu(gate_ref[rows, ls])).astype(BF16)

    @pl.when(rb == 0)
    def _():
        for pair in range(NA_PAIRS_PER_STEP):
            p, inv = _softmax_stats([_dot_nt(stacked_queries(pair), kx_ref[:, lss[pair]])])
            finish(pair, _dot(p[0], vx_ref[:, lss[pair]]) * inv)

    @pl.when(rb > 0)
    def _():
        for j, (k_ref, v_ref) in enumerate(((kp_ref, vp_ref), (kc_ref, vc_ref), (kn_ref, vn_ref))):
            kwin_ref[j * NA_QBLK:(j + 1) * NA_QBLK, :] = k_ref[...]
            vwin_ref[j * NA_QBLK:(j + 1) * NA_QBLK, :] = v_ref[...]
        r0 = (rb - 1) * NA_QROWS
        wins, variants = [], []
        for qr in range(NA_QROWS):
            r = r0 + qr
            start = jnp.clip(r - WIN_R // 2, 0, GRID_H - WIN_R)
            variants.append(start - r + (WIN_R - 1))
            wins.append(pl.ds(pl.multiple_of((start - (r0 - NA_QROWS)) * GRID_W, GRID_W), _NA_WIN_KEYS))
        srows = [slice(qr * n2, (qr + 1) * n2) for qr in range(NA_QROWS)]
        scores = []
        for pair in range(NA_PAIRS_PER_STEP):
            q = stacked_queries(pair)
            s_win = jnp.concatenate(
                [_dot_nt(q[srows[qr]], kwin_ref[wins[qr], lss[pair]]) + bias_ref[pair, variants[qr]]
                 for qr in range(NA_QROWS)], axis=0)
            scores.append([s_win, _dot_nt(q, kx_ref[:, lss[pair]])])
        stats = [_softmax_stats(sc) for sc in scores]
        for pair, (p, inv) in enumerate(stats):
            o_win = jnp.concatenate(
                [_dot(p[0][srows[qr]], vwin_ref[wins[qr], lss[pair]]) for qr in range(NA_QROWS)], axis=0)
            finish(pair, (o_win + _dot(p[1], vx_ref[:, lss[pair]])) * inv)


def _na_attention(qkv, gate, bias):
    nq = NA_DI // _NA_STEP_W

    def blk(row_fn, section):
        return pl.BlockSpec((NA_QBLK, _NA_STEP_W), lambda hq, rb: (row_fn(rb), section * nq + hq))

    prev_rb = lambda rb: jnp.maximum(rb - 1, 0)
    next_rb = lambda rb: jnp.minimum(rb + 1, _NA_BLOCKS - 1)
    same_rb = lambda rb: rb
    ctx_rb = lambda rb: 0
    return pl.pallas_call(
        _na_kernel,
        name="na_attention",
        grid=(nq, _NA_BLOCKS),
        in_specs=[
            blk(same_rb, 0),
            blk(prev_rb, 1), blk(same_rb, 1), blk(next_rb, 1), blk(ctx_rb, 1),
            blk(prev_rb, 2), blk(same_rb, 2), blk(next_rb, 2), blk(ctx_rb, 2),
            pl.BlockSpec((NA_QBLK, _NA_STEP_W), lambda hq, rb: (rb, hq)),
            pl.BlockSpec((NA_PAIRS_PER_STEP, _NA_BIAS_VARIANTS, 2 * GRID_W, _NA_WIN_KEYS),
                         lambda hq, rb: (hq, 0, 0, 0)),
        ],
        out_specs=pl.BlockSpec((NA_QBLK, _NA_STEP_W), lambda hq, rb: (rb, hq)),
        out_shape=jax.ShapeDtypeStruct((ROWS, NA_DI), BF16),
        scratch_shapes=[pltpu.VMEM((_NA_WIN_ROWS * GRID_W, _NA_STEP_W), BF16),
                        pltpu.VMEM((_NA_WIN_ROWS * GRID_W, _NA_STEP_W), BF16)],
        compiler_params=pltpu.CompilerParams(vmem_limit_bytes=VMEM_LIMIT),
    )(qkv, qkv, qkv, qkv, qkv, qkv, qkv, qkv, qkv, gate, bias)


def _na_out_kernel(a_ref, w_ref, r_ref, pg_ref, mod_ref, out_ref):
    _postnorm_residual(_dot(a_ref[...], w_ref[0]), r_ref[...], pg_ref, mod_ref, out_ref)


def _na_out(a, w_out, w_layer, r, post_g, mod, layer, latent_only):
    if latent_only:
        n_ctx = CTX_LEN // ROW_TILE
        out_spec = pl.BlockSpec((ROW_TILE, D_MODEL), lambda i: (jnp.maximum(i - n_ctx, 0), 0))
        out_rows = SEQ
    else:
        out_spec, out_rows = _row_spec(D_MODEL), ROWS
    return pl.pallas_call(
        _na_out_kernel,
        name="na_out",
        grid=(ROWS // ROW_TILE,),
        in_specs=[
            _row_spec(NA_DI),
            _layer_spec((NA_DI, D_MODEL), w_layer),
            _row_spec(D_MODEL),
            _const_spec((1, D_MODEL)),
            _layer_spec((8, 3 * D_MODEL), layer),
        ],
        out_specs=out_spec,
        out_shape=jax.ShapeDtypeStruct((out_rows, D_MODEL), F32),
        compiler_params=pltpu.CompilerParams(
            dimension_semantics=("arbitrary",), vmem_limit_bytes=VMEM_LIMIT),
    )(a, w_out, r, post_g, mod)


def _pad_lanes_row(v):
    flat = v.astype(F32).reshape(1, 2 * SSD_HEADS)
    return jnp.pad(flat, ((0, 0), (0, SSD_DT_PAD - 2 * SSD_HEADS)))


def kernel(x, c, ctx, c_ctx, ada_w, ada_b, pre_g, post_g, ssd_w_in, ssd_conv_w, ssd_conv_b,
           ssd_dt_bias, ssd_a_log, ssd_d, ssd_norm_g, ssd_w_out, na_w_in, na_rpb, na_w_out):
    assert x.shape == (1, SEQ, D_MODEL) and ctx.shape == (1, CTX_LEN, D_MODEL)
    r = (ctx[0], x[0])
    cvec = jnp.zeros((8, D_MODEL), F32).at[0].set(c[0]).at[1].set(c_ctx)
    mod = _modulation(cvec, ada_w, ada_b)
    ssd_w_z_b, ssd_w_out_b = ssd_w_in[:, :, :SSD_DI].astype(BF16), ssd_w_out.astype(BF16)
    na_w_out_b = na_w_out.astype(BF16)
    dt_pad = SSD_DT_PAD - 2 * SSD_HEADS
    na_bias = _na_bias_table(na_rpb)

    for i in range(DEPTH):
        j = i // 2
        pg = pre_g[i].reshape(1, D_MODEL)
        qg = post_g[i].reshape(1, D_MODEL)
        if i % 2 == 0:
            w_dt_t = jnp.pad(ssd_w_in[j, :, SSD_DI + SSD_CONV_CH:].T.astype(BF16), ((0, dt_pad), (0, 0)))
            dtb_row = _pad_lanes_row(ssd_dt_bias[j])
            alog_row = _pad_lanes_row(ssd_a_log[j])
            xs, bc, dt, dtt = _ssd_in(
                r, pg, mod, i, ssd_w_in, j, w_dt_t, ssd_conv_w[j],
                ssd_conv_b[j].reshape(1, SSD_CONV_CH), dtb_row, dtb_row.reshape(SSD_DT_PAD, 1))
            yf, yb = _ssd_scan(
                xs, bc, dt, dtt, alog_row, alog_row.reshape(SSD_DT_PAD, 1),
                jnp.repeat(ssd_d[j].astype(F32), SSD_HEADDIM).reshape(1, SSD_DI))
            r = _ssd_out(yf, yb, ssd_w_z_b, ssd_norm_g[j].reshape(1, SSD_DI), ssd_w_out_b, j,
                         r, pg, qg, mod, i)
        else:
            qkv, gate = _na_in(r, pg, mod, i, na_w_in, j)
            a = _na_attention(qkv, gate, na_bias[j])
            r = _na_out(a, na_w_out_b, j, r, qg, mod, i, latent_only=(i == DEPTH - 1))
    return r[None]
```

```python
import functools

import jax
import jax.numpy as jnp
import numpy as np
from jax import lax
from jax.experimental import pallas as pl
from jax.experimental.pallas import tpu as pltpu

D_MODEL = 1024
SEQ = 16384
DEPTH = 4
GRID_W = 64
GRID_H = SEQ // GRID_W
CTX_LEN = 256
ROWS = CTX_LEN + SEQ

SSD_DI = 2048
SSD_HEADDIM = 64
SSD_HEADS = 32
SSD_STATE = 128
SSD_GROUPS = 8
SSD_HEADS_PER_GROUP = SSD_HEADS // SSD_GROUPS
SSD_GROUP_W = SSD_DI // SSD_GROUPS
SSD_GN = SSD_GROUPS * SSD_STATE
SSD_CONV_CH = SSD_DI + 2 * SSD_GN
SSD_IN = SSD_DI + SSD_CONV_CH + 2 * SSD_HEADS
SSD_DT_PAD = 128

NA_HEADS = 16
NA_HEADDIM = 64
NA_DI = 1024
WIN_R = 8
WIN_C = 16

RMS_EPS = 1e-6
NEG_BIG = -1e30
LOG2_E = 1.4426950408889634

LANES = 128
ROW_TILE = 256
SSD_CHUNK = 128
NA_QROWS = 4
NA_QBLK = NA_QROWS * GRID_W
NA_PAIRS_PER_STEP = 8
VMEM_LIMIT = 56 * 1024 * 1024

F32 = jnp.float32
BF16 = jnp.bfloat16


def _silu(x):
    return x / (1.0 + jnp.exp(-x))


def _softplus(x):
    return jnp.maximum(x, 0.0) + jnp.log(1.0 + jnp.exp(-jnp.abs(x)))


def _dot(a, b):
    return jnp.dot(a, b, preferred_element_type=F32)


def _dot_nt(a, b):
    return lax.dot_general(a, b, (((1,), (1,)), ((), ())), preferred_element_type=F32)


def _split3(v):
    hi = v.astype(BF16)
    r1 = v - hi.astype(F32)
    mid = r1.astype(BF16)
    lo = (r1 - mid.astype(F32)).astype(BF16)
    return hi, mid, lo


def _dot_exact_rhs(sel, v):
    hi, mid, lo = _split3(v)
    return _dot(sel, hi) + _dot(sel, mid) + _dot(sel, lo)


def _dot_exact_lhs(v, sel):
    hi, mid, lo = _split3(v)
    return _dot(hi, sel) + _dot(mid, sel) + _dot(lo, sel)


def _mod_kernel(c_ref, w_ref, b_ref, o_ref):
    s = _silu(c_ref[...]).astype(BF16)
    o_ref[0] = _dot(s, w_ref[0].astype(BF16)) + b_ref[0]


def _modulation(cvec, ada_w, ada_b):
    tn = 1024
    return pl.pallas_call(
        _mod_kernel,
        name="modulation",
        grid=(DEPTH, 3 * D_MODEL // tn),
        in_specs=[
            pl.BlockSpec((8, D_MODEL), lambda i, j: (0, 0)),
            pl.BlockSpec((1, D_MODEL, tn), lambda i, j: (i, 0, j)),
            pl.BlockSpec((1, 1, tn), lambda i, j: (i, 0, j)),
        ],
        out_specs=pl.BlockSpec((1, 8, tn), lambda i, j: (i, 0, j)),
        out_shape=jax.ShapeDtypeStruct((DEPTH, 8, 3 * D_MODEL), F32),
        compiler_params=pltpu.CompilerParams(vmem_limit_bytes=VMEM_LIMIT),
    )(cvec, ada_w, ada_b.reshape(DEPTH, 1, 3 * D_MODEL))


def _mod_rows(mod_ref, is_ctx):
    m = mod_ref[0]
    row = jnp.where(is_ctx, m[1:2, :], m[0:1, :])
    return row[:, :D_MODEL], row[:, D_MODEL:2 * D_MODEL], row[:, 2 * D_MODEL:]


def _is_ctx_tile():
    return pl.program_id(0) * ROW_TILE < CTX_LEN


def _prenorm(x, g_ref, mod_ref):
    is_ctx = _is_ctx_tile()
    shift, scale, _ = _mod_rows(mod_ref, is_ctx)
    y = x * lax.rsqrt(jnp.mean(x * x, axis=-1, keepdims=True) + RMS_EPS) * g_ref[...]
    return (y * (1.0 + scale) + shift).astype(BF16)


def _postnorm_residual(o, r, g_ref, mod_ref, out_ref):
    _, _, gate = _mod_rows(mod_ref, _is_ctx_tile())
    y = o * lax.rsqrt(jnp.mean(o * o, axis=-1, keepdims=True) + RMS_EPS) * g_ref[...]
    out_ref[...] = r + gate * y


def _row_spec(width):
    return pl.BlockSpec((ROW_TILE, width), lambda i: (i, 0))


def _const_spec(shape):
    nd = len(shape)
    return pl.BlockSpec(shape, lambda i: (0,) * nd)


def _layer_spec(shape, layer, single_buffer=False):
    nd = len(shape)
    mode = dict(pipeline_mode=pl.Buffered(1)) if single_buffer else {}
    return pl.BlockSpec((1,) + shape, lambda i: (layer,) + (0,) * nd, **mode)


_COL_CHUNK = 1024
_HALO = 8
_N_ROW_TILES = ROWS // ROW_TILE
_CTX_ROW_TILES = CTX_LEN // ROW_TILE


def _residual_tile(split, refs):
    if not split:
        return refs[0][...], refs[1:]
    return jnp.where(_is_ctx_tile(), refs[0][...], refs[1][...]), refs[2:]


def _residual_specs(split):
    if not split:
        return [_row_spec(D_MODEL)]
    return [pl.BlockSpec((CTX_LEN, D_MODEL), lambda i: (0, 0)),
            pl.BlockSpec((ROW_TILE, D_MODEL), lambda i: (jnp.maximum(i - _CTX_ROW_TILES, 0), 0))]


def _ssd_in_kernel(*refs, split):
    x, refs = _residual_tile(split, refs)
    (xp_ref, xn_ref, g_ref, mod_ref, w_ref, wdtt_ref, cw_ref, cb_ref,
     dtb_row_ref, dtb_col_ref, xs_ref, bc_ref, dt_ref, dtt_ref) = refs
    i = pl.program_id(0)
    h = _prenorm(x, g_ref, mod_ref)
    h_ext = jnp.concatenate([_prenorm(xp_ref[...], g_ref, mod_ref), h,
                             _prenorm(xn_ref[...], g_ref, mod_ref)], axis=0)
    first = (i == 0) | (i == _CTX_ROW_TILES)
    last = (i == _CTX_ROW_TILES - 1) | (i == _N_ROW_TILES - 1)
    rid = lax.broadcasted_iota(jnp.int32, (ROW_TILE, 1), 0)
    keep_up = jnp.logical_not(first & (rid == 0))
    keep_dn = jnp.logical_not(last & (rid == ROW_TILE - 1))
    n_ext = ROW_TILE + 2 * _HALO

    for c in range(0, SSD_CONV_CH, _COL_CHUNK):
        cs = slice(c, c + _COL_CHUNK)
        t = _dot(h_ext, w_ref[0, :, SSD_DI + c:SSD_DI + c + _COL_CHUNK])
        raw = t[_HALO:_HALO + ROW_TILE]
        up = jnp.where(keep_up, pltpu.roll(t, 1, axis=0)[_HALO:_HALO + ROW_TILE], 0.0)
        dn = jnp.where(keep_dn, pltpu.roll(t, n_ext - 1, axis=0)[_HALO:_HALO + ROW_TILE], 0.0)
        act = _silu(cw_ref[0:1, cs] * up + cw_ref[1:2, cs] * raw + cw_ref[2:3, cs] * dn
                    + cb_ref[:, cs])
        if c < SSD_DI:
            xs_ref[:, cs] = act
        else:
            bc_ref[:, c - SSD_DI:c - SSD_DI + _COL_CHUNK] = act.astype(BF16)
    n_dt = 2 * SSD_HEADS
    dt_ref[:, :n_dt] = _softplus(_dot(h, w_ref[0, :, SSD_DI + SSD_CONV_CH:]) + dtb_row_ref[:, :n_dt])
    dt_ref[:, n_dt:] = jnp.zeros((ROW_TILE, SSD_DT_PAD - n_dt), F32)
    dtt_ref[...] = _softplus(_dot_nt(wdtt_ref[...], h) + dtb_col_ref[...])


def _ssd_in(resid, pre_g, mod, layer, w_in, w_layer, w_dt_t, conv_w, conv_b, dtb_row, dtb_col):
    split = isinstance(resid, tuple)
    per8 = ROW_TILE // _HALO
    if split:
        halo_src = resid[1]
        last8 = SEQ // _HALO - 1
        prev8 = lambda i: (jnp.clip((i - _CTX_ROW_TILES) * per8 - 1, 0, last8), 0)
        next8 = lambda i: (jnp.clip((i - _CTX_ROW_TILES + 1) * per8, 0, last8), 0)
        resid_args = list(resid)
    else:
        halo_src = resid
        last8 = ROWS // _HALO - 1
        prev8 = lambda i: (jnp.maximum(i * per8 - 1, 0), 0)
        next8 = lambda i: (jnp.minimum((i + 1) * per8, last8), 0)
        resid_args = [resid]
    return pl.pallas_call(
        functools.partial(_ssd_in_kernel, split=split),
        name="ssd_in",
        grid=(_N_ROW_TILES,),
        in_specs=_residual_specs(split) + [
            pl.BlockSpec((_HALO, D_MODEL), prev8),
            pl.BlockSpec((_HALO, D_MODEL), next8),
            _const_spec((1, D_MODEL)),
            _layer_spec((8, 3 * D_MODEL), layer),
            _layer_spec((D_MODEL, SSD_IN), w_layer, single_buffer=True),
            _const_spec((SSD_DT_PAD, D_MODEL)),
            _const_spec((3, SSD_CONV_CH)),
            _const_spec((1, SSD_CONV_CH)),
            _const_spec((1, SSD_DT_PAD)),
            _const_spec((SSD_DT_PAD, 1)),
        ],
        out_specs=[
            _row_spec(SSD_DI),
            _row_spec(2 * SSD_GN),
            _row_spec(SSD_DT_PAD),
            pl.BlockSpec((SSD_DT_PAD, ROW_TILE), lambda i: (0, i)),
        ],
        out_shape=[
            jax.ShapeDtypeStruct((ROWS, SSD_DI), F32),
            jax.ShapeDtypeStruct((ROWS, 2 * SSD_GN), BF16),
            jax.ShapeDtypeStruct((ROWS, SSD_DT_PAD), F32),
            jax.ShapeDtypeStruct((SSD_DT_PAD, ROWS), F32),
        ],
        compiler_params=pltpu.CompilerParams(vmem_limit_bytes=VMEM_LIMIT),
    )(*resid_args, halo_src, halo_src, pre_g, mod, w_in, w_dt_t,
      conv_w, conv_b, dtb_row, dtb_col)


_N_CHUNKS = ROWS // SSD_CHUNK
_N_CTX_CHUNKS = CTX_LEN // SSD_CHUNK


def _fwd_chunk(i):
    return i


def _bwd_chunk(i):
    return jnp.where(i < _N_CTX_CHUNKS, _N_CTX_CHUNKS - 1 - i, _N_CHUNKS - 1 + _N_CTX_CHUNKS - i)


def _ssd_direction(d, xs_ref, bc_ref, dt_ref, dtt_ref, alog_row_ref, alog_col_ref,
                   dskip_ref, state_ref, y_ref):
    t = SSD_CHUNK
    xs = xs_ref[...]
    xs_b = xs.astype(BF16)

    dt_head = dtt_ref[...]
    a_tok = dt_ref[...] * (-LOG2_E * jnp.exp(alog_row_ref[...]))
    a_head = dt_head * (-LOG2_E * jnp.exp(alog_col_ref[...]))

    li = lax.broadcasted_iota(jnp.int32, (t, t), 0)
    si = lax.broadcasted_iota(jnp.int32, (t, t), 1)
    lower = si <= li
    upper = si >= li
    mask = lower if d == 0 else upper
    tri = jnp.where(mask, 1.0, 0.0).astype(BF16)
    tri_t = jnp.where(upper if d == 0 else lower, 1.0, 0.0).astype(BF16)

    cum = _dot_exact_rhs(tri, a_tok)
    cum_t = _dot_exact_lhs(a_head, tri_t)
    end = t - 1 if d == 0 else 0

    tot_col = cum_t[:, end:end + 1]
    w_head = jnp.exp2(tot_col - cum_t) * dt_head
    src_t = cum_t - jnp.log2(dt_head)

    lane = lax.broadcasted_iota(jnp.int32, (1, LANES), 1)
    first_head = lane < SSD_HEADDIM

    lhs_y, lhs_s, e_in = [], [], []
    for g in range(SSD_GROUPS):
        b_g = bc_ref[:, g * SSD_STATE:(g + 1) * SSD_STATE]
        c_g = bc_ref[:, SSD_GN + g * SSD_STATE:SSD_GN + (g + 1) * SSD_STATE]
        bt_g = b_g.astype(F32).T
        cb = _dot_nt(c_g, b_g)
        for pair in range(SSD_HEADS_PER_GROUP // 2):
            m, btw, e_col = [], [], []
            for k in range(2):
                idx = d * SSD_HEADS + g * SSD_HEADS_PER_GROUP + 2 * pair + k
                col = jnp.broadcast_to(cum[:, idx:idx + 1], (t, LANES))
                m.append((cb * jnp.exp2(jnp.where(mask, col - src_t[idx:idx + 1, :], NEG_BIG))
                          ).astype(BF16))
                btw.append((bt_g * w_head[idx:idx + 1, :]).astype(BF16))
                e_col.append(col)
            lhs_y.append(jnp.concatenate(m, axis=1))
            lhs_s.append(jnp.concatenate(btw, axis=1))
            e_in.append(jnp.exp2(jnp.where(first_head, e_col[0], e_col[1])))
    for g in range(SSD_GROUPS):
        c_g = bc_ref[:, SSD_GN + g * SSD_STATE:SSD_GN + (g + 1) * SSD_STATE]
        for pair in range(SSD_HEADS_PER_GROUP // 2):
            n = g * (SSD_HEADS_PER_GROUP // 2) + pair
            cs = slice(g * SSD_GROUP_W + pair * LANES, g * SSD_GROUP_W + (pair + 1) * LANES)
            ps = slice(pair * LANES, (pair + 1) * LANES)
            x_pair = xs_b[:, cs]
            zero_x = jnp.zeros_like(x_pair)
            x_stack = jnp.concatenate([jnp.where(first_head, x_pair, zero_x),
                                       jnp.where(first_head, zero_x, x_pair)], axis=0)
            state = state_ref[d, g, :, ps]
            y = _dot(lhs_y[n], x_stack) + e_in[n] * _dot(c_g, state.astype(BF16))
            state_ref[d, g, :, ps] = state * e_in[n][end:end + 1, :] + _dot(lhs_s[n], x_stack)
            if d == 0:
                y = y + xs[:, cs] * dskip_ref[:, cs]
            y_ref[:, cs] = y


def _ssd_scan_kernel(xs_f, bc_f, dt_f, dtt_f, xs_b, bc_b, dt_b, dtt_b,
                     alog_row_ref, alog_col_ref, dskip_ref, yf_ref, yb_ref, state_ref):
    @pl.when(pl.program_id(0) == 0)
    def _():
        state_ref[...] = jnp.zeros_like(state_ref)

    params = (alog_row_ref, alog_col_ref, dskip_ref, state_ref)
    _ssd_direction(0, xs_f, bc_f, dt_f, dtt_f, *params, yf_ref)
    _ssd_direction(1, xs_b, bc_b, dt_b, dtt_b, *params, yb_ref)


def _ssd_scan(xs, bc, dt, dtt, alog_row, alog_col, dskip):
    t = SSD_CHUNK

    def chunk_specs(order):
        return [
            pl.BlockSpec((t, SSD_DI), lambda i: (order(i), 0)),
            pl.BlockSpec((t, 2 * SSD_GN), lambda i: (order(i), 0)),
            pl.BlockSpec((t, SSD_DT_PAD), lambda i: (order(i), 0)),
            pl.BlockSpec((SSD_DT_PAD, t), lambda i: (0, order(i))),
        ]

    y_shape = jax.ShapeDtypeStruct((ROWS, SSD_DI), F32)
    return pl.pallas_call(
        _ssd_scan_kernel,
        name="ssd_scan",
        grid=(_N_CHUNKS,),
        in_specs=chunk_specs(_fwd_chunk) + chunk_specs(_bwd_chunk) + [
            _const_spec((1, SSD_DT_PAD)),
            _const_spec((SSD_DT_PAD, 1)),
            _const_spec((1, SSD_DI)),
        ],
        out_specs=[
            pl.BlockSpec((t, SSD_DI), lambda i: (_fwd_chunk(i), 0)),
            pl.BlockSpec((t, SSD_DI), lambda i: (_bwd_chunk(i), 0)),
        ],
        out_shape=[y_shape, y_shape],
        scratch_shapes=[pltpu.VMEM((2, SSD_GROUPS, SSD_STATE, SSD_GROUP_W), F32)],
        compiler_params=pltpu.CompilerParams(
            dimension_semantics=("arbitrary",), vmem_limit_bytes=VMEM_LIMIT),
    )(xs, bc, dt, dtt, xs, bc, dt, dtt, alog_row, alog_col, dskip)


def _ssd_out_kernel(yf_ref, yb_ref, wz_ref, ng_ref, w_ref, pre_g_ref, pg_ref, mod_ref, *refs, split):
    r, (out_ref,) = _residual_tile(split, refs)
    h = _prenorm(r, pre_g_ref, mod_ref)
    parts = []
    for g in range(SSD_GROUPS):
        gs = slice(g * SSD_GROUP_W, (g + 1) * SSD_GROUP_W)
        yg = (yf_ref[:, gs] + yb_ref[:, gs]) * _silu(_dot(h, wz_ref[0, :, gs]))
        parts.append(yg * lax.rsqrt(jnp.mean(yg * yg, axis=-1, keepdims=True) + RMS_EPS))
    yn = (jnp.concatenate(parts, axis=1) * ng_ref[...]).astype(BF16)
    _postnorm_residual(_dot(yn, w_ref[0]), r, pg_ref, mod_ref, out_ref)


def _ssd_out(yf, yb, w_in, norm_g, w_out, w_layer, resid, pre_g, post_g, mod, layer):
    split = isinstance(resid, tuple)
    return pl.pallas_call(
        functools.partial(_ssd_out_kernel, split=split),
        name="ssd_out",
        grid=(ROWS // ROW_TILE,),
        in_specs=[
            _row_spec(SSD_DI), _row_spec(SSD_DI),
            _layer_spec((D_MODEL, SSD_DI), w_layer),
            _const_spec((1, SSD_DI)),
            _layer_spec((SSD_DI, D_MODEL), w_layer),
            _const_spec((1, D_MODEL)),
            _const_spec((1, D_MODEL)),
            _layer_spec((8, 3 * D_MODEL), layer),
        ] + _residual_specs(split),
        out_specs=_row_spec(D_MODEL),
        out_shape=jax.ShapeDtypeStruct((ROWS, D_MODEL), F32),
        compiler_params=pltpu.CompilerParams(vmem_limit_bytes=VMEM_LIMIT),
    )(yf, yb, w_in, norm_g, w_out, pre_g, post_g, mod, *(resid if split else (resid,)))


def _na_in_kernel(x_ref, g_ref, mod_ref, w_ref, qkv_ref, gate_ref):
    h = _prenorm(x_ref[...], g_ref, mod_ref)
    scale = NA_HEADDIM ** -0.5 * LOG2_E
    qkv_ref[:, :NA_DI] = (_dot(h, w_ref[0, :, :NA_DI]) * scale).astype(BF16)
    for c in range(NA_DI, 3 * NA_DI, _COL_CHUNK):
        qkv_ref[:, c:c + _COL_CHUNK] = _dot(h, w_ref[0, :, c:c + _COL_CHUNK]).astype(BF16)
    gate_ref[...] = _dot(h, w_ref[0, :, 3 * NA_DI:])


def _na_in(r, pre_g, mod, layer, w_in, w_layer):
    return pl.pallas_call(
        _na_in_kernel,
        name="na_in",
        grid=(ROWS // ROW_TILE,),
        in_specs=[
            _row_spec(D_MODEL),
            _const_spec((1, D_MODEL)),
            _layer_spec((8, 3 * D_MODEL), layer),
            _layer_spec((D_MODEL, 4 * NA_DI), w_layer, single_buffer=True),
        ],
        out_specs=[_row_spec(3 * NA_DI), _row_spec(NA_DI)],
        out_shape=[
            jax.ShapeDtypeStruct((ROWS, 3 * NA_DI), BF16),
            jax.ShapeDtypeStruct((ROWS, NA_DI), F32),
        ],
        compiler_params=pltpu.CompilerParams(vmem_limit_bytes=VMEM_LIMIT),
    )(r, pre_g, mod, w_in)


_NA_BLOCKS = ROWS // NA_QBLK
_NA_WIN_ROWS = 3 * NA_QROWS
_NA_WIN_KEYS = WIN_R * GRID_W
_NA_BIAS_VARIANTS = WIN_R
_NA_STEP_W = NA_PAIRS_PER_STEP * LANES


def _na_bias_table(rpb):
    n = GRID_W
    cols = np.arange(n)
    col_start = np.clip(cols - WIN_C // 2, 0, n - WIN_C)
    cvalid = (cols[None, :] >= col_start[:, None]) & (cols[None, :] < col_start[:, None] + WIN_C)

    rpb = rpb.astype(F32)
    edge = n - WIN_C
    ext = jnp.concatenate([jnp.repeat(rpb[..., :1], edge, axis=-1), rpb,
                           jnp.repeat(rpb[..., -1:], edge + 1, axis=-1)], axis=-1)
    skew = jnp.tile(ext, (1, 1, n))[..., :n * (2 * n - 1)].reshape(NA_HEADS, 2 * WIN_R - 1, n, 2 * n - 1)
    toep = jnp.where(cvalid, skew[..., n - 1:] * LOG2_E, NEG_BIG)
    rows = toep.transpose(0, 2, 1, 3).reshape(NA_HEADS, n, (2 * WIN_R - 1) * n)
    table = jnp.stack([rows[:, :, v * n:v * n + _NA_WIN_KEYS] for v in range(_NA_BIAS_VARIANTS)],
                      axis=1)
    return table.reshape(NA_HEADS // 2, 2, _NA_BIAS_VARIANTS, n, _NA_WIN_KEYS).transpose(
        0, 2, 1, 3, 4).reshape(NA_HEADS // 2, _NA_BIAS_VARIANTS, 2 * n, _NA_WIN_KEYS)


def _softmax_stats(scores):
    m = functools.reduce(jnp.maximum, [jnp.max(x, axis=-1, keepdims=True) for x in scores])
    p = [jnp.exp2(x - m) for x in scores]
    inv = 1.0 / functools.reduce(jnp.add, [jnp.sum(x, axis=-1, keepdims=True) for x in p])
    return [x.astype(BF16) for x in p], inv


def _na_kernel(q_ref, kp_ref, kc_ref, kn_ref, kx_ref, vp_ref, vc_ref, vn_ref, vx_ref,
               gate_ref, bias_ref, o_ref, kwin_ref, vwin_ref):
    rb = pl.program_id(1)
    lane = lax.broadcasted_iota(jnp.int32, (1, LANES), 1)
    first_head = lane < NA_HEADDIM
    lss = [slice(pair * LANES, (pair + 1) * LANES) for pair in range(NA_PAIRS_PER_STEP)]
    n, n2 = GRID_W, 2 * GRID_W

    def stacked_queries(pair):
        q = q_ref[:, lss[pair]]
        zero = jnp.zeros_like(q)
        q0 = jnp.where(first_head, q, zero)
        q1 = jnp.where(first_head, zero, q)
        parts = []
        for qr in range(NA_QROWS):
            parts += [q0[qr * n:(qr + 1) * n], q1[qr * n:(qr + 1) * n]]
        return jnp.concatenate(parts, axis=0)

    def finish(pair, o):
        ls = lss[pair]
        for qr in range(NA_QROWS):
            rows = slice(qr * n, (qr + 1) * n)
            merged = jnp.where(first_head, o[qr * n2:qr * n2 + n], o[qr * n2 + n:(qr + 1) * n2])
            o_ref[rows, ls] = (merged * _silu(gate_ref[rows, ls])).astype(BF16)

    @pl.when(rb == 0)
    def _():
        for pair in range(NA_PAIRS_PER_STEP):
            p, inv = _softmax_stats([_dot_nt(stacked_queries(pair), kx_ref[:, lss[pair]])])
            finish(pair, _dot(p[0], vx_ref[:, lss[pair]]) * inv)

    @pl.when(rb > 0)
    def _():
        for j, (k_ref, v_ref) in enumerate(((kp_ref, vp_ref), (kc_ref, vc_ref), (kn_ref, vn_ref))):
            kwin_ref[j * NA_QBLK:(j + 1) * NA_QBLK, :] = k_ref[...]
            vwin_ref[j * NA_QBLK:(j + 1) * NA_QBLK, :] = v_ref[...]
        r0 = (rb - 1) * NA_QROWS
        wins, variants = [], []
        for qr in range(NA_QROWS):
            r = r0 + qr
            start = jnp.clip(r - WIN_R // 2, 0, GRID_H - WIN_R)
            variants.append(start - r + (WIN_R - 1))
            wins.append(pl.ds(pl.multiple_of((start - (r0 - NA_QROWS)) * GRID_W, GRID_W), _NA_WIN_KEYS))
        srows = [slice(qr * n2, (qr + 1) * n2) for qr in range(NA_QROWS)]
        scores = []
        for pair in range(NA_PAIRS_PER_STEP):
            q = stacked_queries(pair)
            s_win = jnp.concatenate(
                [_dot_nt(q[srows[qr]], kwin_ref[wins[qr], lss[pair]]) + bias_ref[pair, variants[qr]]
                 for qr in range(NA_QROWS)], axis=0)
            scores.append([s_win, _dot_nt(q, kx_ref[:, lss[pair]])])
        stats = [_softmax_stats(sc) for sc in scores]
        for pair, (p, inv) in enumerate(stats):
            o_win = jnp.concatenate(
                [_dot(p[0][srows[qr]], vwin_ref[wins[qr], lss[pair]]) for qr in range(NA_QROWS)], axis=0)
            finish(pair, (o_win + _dot(p[1], vx_ref[:, lss[pair]])) * inv)


def _na_attention(qkv, gate, bias):
    nq = NA_DI // _NA_STEP_W

    def blk(row_fn, section):
        return pl.BlockSpec((NA_QBLK, _NA_STEP_W), lambda hq, rb: (row_fn(rb), section * nq + hq))

    prev_rb = lambda rb: jnp.maximum(rb - 1, 0)
    next_rb = lambda rb: jnp.minimum(rb + 1, _NA_BLOCKS - 1)
    same_rb = lambda rb: rb
    ctx_rb = lambda rb: 0
    return pl.pallas_call(
        _na_kernel,
        name="na_attention",
        grid=(nq, _NA_BLOCKS),
        in_specs=[
            blk(same_rb, 0),
            blk(prev_rb, 1), blk(same_rb, 1), blk(next_rb, 1), blk(ctx_rb, 1),
            blk(prev_rb, 2), blk(same_rb, 2), blk(next_rb, 2), blk(ctx_rb, 2),
            pl.BlockSpec((NA_QBLK, _NA_STEP_W), lambda hq, rb: (rb, hq)),
            pl.BlockSpec((NA_PAIRS_PER_STEP, _NA_BIAS_VARIANTS, 2 * GRID_W, _NA_WIN_KEYS),
                         lambda hq, rb: (hq, 0, 0, 0), pipeline_mode=pl.Buffered(1)),
        ],
        out_specs=pl.BlockSpec((NA_QBLK, _NA_STEP_W), lambda hq, rb: (rb, hq)),
        out_shape=jax.ShapeDtypeStruct((ROWS, NA_DI), BF16),
        scratch_shapes=[pltpu.VMEM((_NA_WIN_ROWS * GRID_W, _NA_STEP_W), BF16),
                        pltpu.VMEM((_NA_WIN_ROWS * GRID_W, _NA_STEP_W), BF16)],
        compiler_params=pltpu.CompilerParams(vmem_limit_bytes=VMEM_LIMIT),
    )(qkv, qkv, qkv, qkv, qkv, qkv, qkv, qkv, qkv, gate, bias)


def _na_out_kernel(a_ref, w_ref, r_ref, pg_ref, mod_ref, out_ref):
    _postnorm_residual(_dot(a_ref[...], w_ref[0]), r_ref[...], pg_ref, mod_ref, out_ref)


def _na_out(a, w_out, w_layer, r, post_g, mod, layer, latent_only):
    if latent_only:
        n_ctx = CTX_LEN // ROW_TILE
        out_spec = pl.BlockSpec((ROW_TILE, D_MODEL), lambda i: (jnp.maximum(i - n_ctx, 0), 0))
        out_rows = SEQ
    else:
        out_spec, out_rows = _row_spec(D_MODEL), ROWS
    return pl.pallas_call(
        _na_out_kernel,
        name="na_out",
        grid=(ROWS // ROW_TILE,),
        in_specs=[
            _row_spec(NA_DI),
            _layer_spec((NA_DI, D_MODEL), w_layer),
            _row_spec(D_MODEL),
            _const_spec((1, D_MODEL)),
            _layer_spec((8, 3 * D_MODEL), layer),
        ],
        out_specs=out_spec,
        out_shape=jax.ShapeDtypeStruct((out_rows, D_MODEL), F32),
        compiler_params=pltpu.CompilerParams(
            dimension_semantics=("arbitrary",), vmem_limit_bytes=VMEM_LIMIT),
    )(a, w_out, r, post_g, mod)


def _pad_lanes_row(v):
    flat = v.astype(F32).reshape(1, 2 * SSD_HEADS)
    return jnp.pad(flat, ((0, 0), (0, SSD_DT_PAD - 2 * SSD_HEADS)))


def kernel(x, c, ctx, c_ctx, ada_w, ada_b, pre_g, post_g, ssd_w_in, ssd_conv_w, ssd_conv_b,
           ssd_dt_bias, ssd_a_log, ssd_d, ssd_norm_g, ssd_w_out, na_w_in, na_rpb, na_w_out):
    assert x.shape == (1, SEQ, D_MODEL) and ctx.shape == (1, CTX_LEN, D_MODEL)
    r = (ctx[0], x[0])
    cvec = jnp.zeros((8, D_MODEL), F32).at[0].set(c[0]).at[1].set(c_ctx)
    mod = _modulation(cvec, ada_w, ada_b)
    ssd_w_in_b, ssd_w_out_b = ssd_w_in.astype(BF16), ssd_w_out.astype(BF16)
    na_w_in_b, na_w_out_b = na_w_in.astype(BF16), na_w_out.astype(BF16)
    dt_pad = SSD_DT_PAD - 2 * SSD_HEADS

    for i in range(DEPTH):
        j = i // 2
        pg = pre_g[i].reshape(1, D_MODEL)
        qg = post_g[i].reshape(1, D_MODEL)
        if i % 2 == 0:
            w_dt_t = jnp.pad(ssd_w_in_b[j, :, SSD_DI + SSD_CONV_CH:].T, ((0, dt_pad), (0, 0)))
            dtb_row = _pad_lanes_row(ssd_dt_bias[j])
            alog_row = _pad_lanes_row(ssd_a_log[j])
            xs, bc, dt, dtt = _ssd_in(
                r, pg, mod, i, ssd_w_in_b, j, w_dt_t, ssd_conv_w[j],
                ssd_conv_b[j].reshape(1, SSD_CONV_CH), dtb_row, dtb_row.reshape(SSD_DT_PAD, 1))
            yf, yb = _ssd_scan(
                xs, bc, dt, dtt, alog_row, alog_row.reshape(SSD_DT_PAD, 1),
                jnp.repeat(ssd_d[j].astype(F32), SSD_HEADDIM).reshape(1, SSD_DI))
            r = _ssd_out(yf, yb, ssd_w_in_b, ssd_norm_g[j].reshape(1, SSD_DI), ssd_w_out_b, j,
                         r, pg, qg, mod, i)
        else:
            qkv, gate = _na_in(r, pg, mod, i, na_w_in_b, j)
            a = _na_attention(qkv, gate, _na_bias_table(na_rpb[j]))
            r = _na_out(a, na_w_out_b, j, r, qg, mod, i, latent_only=(i == DEPTH - 1))
    return r[None]
```

```python
import functools

import jax
import jax.numpy as jnp
import numpy as np
from jax import lax
from jax.experimental import pallas as pl
from jax.experimental.pallas import tpu as pltpu

D_MODEL = 1024
SEQ = 16384
DEPTH = 4
GRID_W = 64
GRID_H = SEQ // GRID_W
CTX_LEN = 256
ROWS = CTX_LEN + SEQ

SSD_DI = 2048
SSD_HEADDIM = 64
SSD_HEADS = 32
SSD_STATE = 128
SSD_GROUPS = 8
SSD_HEADS_PER_GROUP = SSD_HEADS // SSD_GROUPS
SSD_GROUP_W = SSD_DI // SSD_GROUPS
SSD_GN = SSD_GROUPS * SSD_STATE
SSD_CONV_CH = SSD_DI + 2 * SSD_GN
SSD_IN = SSD_DI + SSD_CONV_CH + 2 * SSD_HEADS
SSD_DT_PAD = 128

NA_HEADS = 16
NA_HEADDIM = 64
NA_DI = 1024
WIN_R = 8
WIN_C = 16

RMS_EPS = 1e-6
NEG_BIG = -1e30
LOG2_E = 1.4426950408889634

LANES = 128
ROW_TILE = 256
SSD_CHUNK = 128
NA_QROWS = 4
NA_QBLK = NA_QROWS * GRID_W
NA_PAIRS_PER_STEP = 8
VMEM_LIMIT = 56 * 1024 * 1024

F32 = jnp.float32
BF16 = jnp.bfloat16


def _silu(x):
    return x / (1.0 + jnp.exp(-x))


def _softplus(x):
    return jnp.maximum(x, 0.0) + jnp.log(1.0 + jnp.exp(-jnp.abs(x)))


def _dot(a, b):
    return jnp.dot(a, b, preferred_element_type=F32)


def _dot_nt(a, b):
    return lax.dot_general(a, b, (((1,), (1,)), ((), ())), preferred_element_type=F32)


def _split3(v):
    hi = v.astype(BF16)
    r1 = v - hi.astype(F32)
    mid = r1.astype(BF16)
    lo = (r1 - mid.astype(F32)).astype(BF16)
    return hi, mid, lo


def _dot_exact_rhs(sel, v):
    hi, mid, lo = _split3(v)
    return _dot(sel, hi) + _dot(sel, mid) + _dot(sel, lo)


def _dot_exact_lhs(v, sel):
    hi, mid, lo = _split3(v)
    return _dot(hi, sel) + _dot(mid, sel) + _dot(lo, sel)


def _mod_kernel(c_ref, w_ref, b_ref, o_ref):
    s = _silu(c_ref[...]).astype(BF16)
    o_ref[0] = _dot(s, w_ref[0].astype(BF16)) + b_ref[0]


def _modulation(cvec, ada_w, ada_b):
    tn = 1024
    return pl.pallas_call(
        _mod_kernel,
        name="modulation",
        grid=(DEPTH, 3 * D_MODEL // tn),
        in_specs=[
            pl.BlockSpec((8, D_MODEL), lambda i, j: (0, 0)),
            pl.BlockSpec((1, D_MODEL, tn), lambda i, j: (i, 0, j)),
            pl.BlockSpec((1, 1, tn), lambda i, j: (i, 0, j)),
        ],
        out_specs=pl.BlockSpec((1, 8, tn), lambda i, j: (i, 0, j)),
        out_shape=jax.ShapeDtypeStruct((DEPTH, 8, 3 * D_MODEL), F32),
        compiler_params=pltpu.CompilerParams(vmem_limit_bytes=VMEM_LIMIT),
    )(cvec, ada_w, ada_b.reshape(DEPTH, 1, 3 * D_MODEL))


def _mod_rows(mod_ref, is_ctx):
    m = mod_ref[0]
    row = jnp.where(is_ctx, m[1:2, :], m[0:1, :])
    return row[:, :D_MODEL], row[:, D_MODEL:2 * D_MODEL], row[:, 2 * D_MODEL:]


def _is_ctx_tile():
    return pl.program_id(0) * ROW_TILE < CTX_LEN


def _prenorm(x, g_ref, mod_ref):
    is_ctx = _is_ctx_tile()
    shift, scale, _ = _mod_rows(mod_ref, is_ctx)
    y = x * lax.rsqrt(jnp.mean(x * x, axis=-1, keepdims=True) + RMS_EPS) * g_ref[...]
    return (y * (1.0 + scale) + shift).astype(BF16)


def _postnorm_residual(o, r, g_ref, mod_ref, out_ref):
    _, _, gate = _mod_rows(mod_ref, _is_ctx_tile())
    y = o * lax.rsqrt(jnp.mean(o * o, axis=-1, keepdims=True) + RMS_EPS) * g_ref[...]
    out_ref[...] = r + gate * y


def _row_spec(width):
    return pl.BlockSpec((ROW_TILE, width), lambda i: (i, 0))


def _const_spec(shape):
    nd = len(shape)
    return pl.BlockSpec(shape, lambda i: (0,) * nd)


def _layer_spec(shape, layer, single_buffer=False):
    nd = len(shape)
    mode = dict(pipeline_mode=pl.Buffered(1)) if single_buffer else {}
    return pl.BlockSpec((1,) + shape, lambda i: (layer,) + (0,) * nd, **mode)


_COL_CHUNK = 1024
_HALO = 8
_N_ROW_TILES = ROWS // ROW_TILE
_CTX_ROW_TILES = CTX_LEN // ROW_TILE


def _residual_tile(split, refs):
    if not split:
        return refs[0][...], refs[1:]
    return jnp.where(_is_ctx_tile(), refs[0][...], refs[1][...]), refs[2:]


def _residual_specs(split):
    if not split:
        return [_row_spec(D_MODEL)]
    return [pl.BlockSpec((CTX_LEN, D_MODEL), lambda i: (0, 0)),
            pl.BlockSpec((ROW_TILE, D_MODEL), lambda i: (jnp.maximum(i - _CTX_ROW_TILES, 0), 0))]


def _ssd_in_kernel(*refs, split):
    x, refs = _residual_tile(split, refs)
    (xp_ref, xn_ref, g_ref, mod_ref, w_ref, wdtt_ref, cw_ref, cb_ref,
     dtb_row_ref, dtb_col_ref, xs_ref, bc_ref, dt_ref, dtt_ref) = refs
    i = pl.program_id(0)
    h = _prenorm(x, g_ref, mod_ref)
    h_ext = jnp.concatenate([_prenorm(xp_ref[...], g_ref, mod_ref), h,
                             _prenorm(xn_ref[...], g_ref, mod_ref)], axis=0)
    first = (i == 0) | (i == _CTX_ROW_TILES)
    last = (i == _CTX_ROW_TILES - 1) | (i == _N_ROW_TILES - 1)
    rid = lax.broadcasted_iota(jnp.int32, (ROW_TILE, 1), 0)
    keep_up = jnp.logical_not(first & (rid == 0))
    keep_dn = jnp.logical_not(last & (rid == ROW_TILE - 1))
    n_ext = ROW_TILE + 2 * _HALO

    for c in range(0, SSD_CONV_CH, _COL_CHUNK):
        cs = slice(c, c + _COL_CHUNK)
        t = _dot(h_ext, w_ref[0, :, SSD_DI + c:SSD_DI + c + _COL_CHUNK])
        raw = t[_HALO:_HALO + ROW_TILE]
        up = jnp.where(keep_up, pltpu.roll(t, 1, axis=0)[_HALO:_HALO + ROW_TILE], 0.0)
        dn = jnp.where(keep_dn, pltpu.roll(t, n_ext - 1, axis=0)[_HALO:_HALO + ROW_TILE], 0.0)
        act = _silu(cw_ref[0:1, cs] * up + cw_ref[1:2, cs] * raw + cw_ref[2:3, cs] * dn
                    + cb_ref[:, cs])
        if c < SSD_DI:
            xs_ref[:, cs] = act
        else:
            bc_ref[:, c - SSD_DI:c - SSD_DI + _COL_CHUNK] = act.astype(BF16)
    n_dt = 2 * SSD_HEADS
    dt_ref[:, :n_dt] = _softplus(_dot(h, w_ref[0, :, SSD_DI + SSD_CONV_CH:]) + dtb_row_ref[:, :n_dt])
    dt_ref[:, n_dt:] = jnp.zeros((ROW_TILE, SSD_DT_PAD - n_dt), F32)
    dtt_ref[...] = _softplus(_dot_nt(wdtt_ref[...], h) + dtb_col_ref[...])


def _ssd_in(resid, pre_g, mod, layer, w_in, w_layer, w_dt_t, conv_w, conv_b, dtb_row, dtb_col):
    split = isinstance(resid, tuple)
    per8 = ROW_TILE // _HALO
    if split:
        halo_src = resid[1]
        last8 = SEQ // _HALO - 1
        prev8 = lambda i: (jnp.clip((i - _CTX_ROW_TILES) * per8 - 1, 0, last8), 0)
        next8 = lambda i: (jnp.clip((i - _CTX_ROW_TILES + 1) * per8, 0, last8), 0)
        resid_args = list(resid)
    else:
        halo_src = resid
        last8 = ROWS // _HALO - 1
        prev8 = lambda i: (jnp.maximum(i * per8 - 1, 0), 0)
        next8 = lambda i: (jnp.minimum((i + 1) * per8, last8), 0)
        resid_args = [resid]
    return pl.pallas_call(
        functools.partial(_ssd_in_kernel, split=split),
        name="ssd_in",
        grid=(_N_ROW_TILES,),
        in_specs=_residual_specs(split) + [
            pl.BlockSpec((_HALO, D_MODEL), prev8),
            pl.BlockSpec((_HALO, D_MODEL), next8),
            _const_spec((1, D_MODEL)),
            _layer_spec((8, 3 * D_MODEL), layer),
            _layer_spec((D_MODEL, SSD_IN), w_layer, single_buffer=True),
            _const_spec((SSD_DT_PAD, D_MODEL)),
            _const_spec((3, SSD_CONV_CH)),
            _const_spec((1, SSD_CONV_CH)),
            _const_spec((1, SSD_DT_PAD)),
            _const_spec((SSD_DT_PAD, 1)),
        ],
        out_specs=[
            _row_spec(SSD_DI),
            _row_spec(2 * SSD_GN),
            _row_spec(SSD_DT_PAD),
            pl.BlockSpec((SSD_DT_PAD, ROW_TILE), lambda i: (0, i)),
        ],
        out_shape=[
            jax.ShapeDtypeStruct((ROWS, SSD_DI), F32),
            jax.ShapeDtypeStruct((ROWS, 2 * SSD_GN), BF16),
            jax.ShapeDtypeStruct((ROWS, SSD_DT_PAD), F32),
            jax.ShapeDtypeStruct((SSD_DT_PAD, ROWS), F32),
        ],
        compiler_params=pltpu.CompilerParams(vmem_limit_bytes=VMEM_LIMIT),
    )(*resid_args, halo_src, halo_src, pre_g, mod, w_in, w_dt_t,
      conv_w, conv_b, dtb_row, dtb_col)


_N_CHUNKS = ROWS // SSD_CHUNK
_N_CTX_CHUNKS = CTX_LEN // SSD_CHUNK


def _fwd_chunk(i):
    return i


def _bwd_chunk(i):
    return jnp.where(i < _N_CTX_CHUNKS, _N_CTX_CHUNKS - 1 - i, _N_CHUNKS - 1 + _N_CTX_CHUNKS - i)


def _ssd_direction(d, xs_ref, bc_ref, dt_ref, dtt_ref, alog_row_ref, alog_col_ref,
                   dskip_ref, state_ref, y_ref):
    t = SSD_CHUNK
    xs = xs_ref[...]
    xs_b = xs.astype(BF16)

    dt_head = dtt_ref[...]
    a_tok = dt_ref[...] * (-LOG2_E * jnp.exp(alog_row_ref[...]))
    a_head = dt_head * (-LOG2_E * jnp.exp(alog_col_ref[...]))

    li = lax.broadcasted_iota(jnp.int32, (t, t), 0)
    si = lax.broadcasted_iota(jnp.int32, (t, t), 1)
    lower = si <= li
    upper = si >= li
    mask = lower if d == 0 else upper
    tri = jnp.where(mask, 1.0, 0.0).astype(BF16)
    tri_t = jnp.where(upper if d == 0 else lower, 1.0, 0.0).astype(BF16)

    cum = _dot_exact_rhs(tri, a_tok)
    cum_t = _dot_exact_lhs(a_head, tri_t)
    end = t - 1 if d == 0 else 0

    tot_col = cum_t[:, end:end + 1]
    w_head = jnp.exp2(tot_col - cum_t) * dt_head
    src_t = cum_t - jnp.log2(dt_head)

    lane = lax.broadcasted_iota(jnp.int32, (1, LANES), 1)
    first_head = lane < SSD_HEADDIM

    lhs_y, lhs_s, e_in = [], [], []
    for g in range(SSD_GROUPS):
        b_g = bc_ref[:, g * SSD_STATE:(g + 1) * SSD_STATE]
        c_g = bc_ref[:, SSD_GN + g * SSD_STATE:SSD_GN + (g + 1) * SSD_STATE]
        bt_g = b_g.astype(F32).T
        cb = _dot_nt(c_g, b_g)
        for pair in range(SSD_HEADS_PER_GROUP // 2):
            m, btw, e_col = [], [], []
            for k in range(2):
                idx = d * SSD_HEADS + g * SSD_HEADS_PER_GROUP + 2 * pair + k
                col = jnp.broadcast_to(cum[:, idx:idx + 1], (t, LANES))
                m.append((cb * jnp.exp2(jnp.where(mask, col - src_t[idx:idx + 1, :], NEG_BIG))
                          ).astype(BF16))
                btw.append((bt_g * w_head[idx:idx + 1, :]).astype(BF16))
                e_col.append(col)
            lhs_y.append(jnp.concatenate(m, axis=1))
            lhs_s.append(jnp.concatenate(btw, axis=1))
            e_in.append(jnp.exp2(jnp.where(first_head, e_col[0], e_col[1])))
    for g in range(SSD_GROUPS):
        c_g = bc_ref[:, SSD_GN + g * SSD_STATE:SSD_GN + (g + 1) * SSD_STATE]
        for pair in range(SSD_HEADS_PER_GROUP // 2):
            n = g * (SSD_HEADS_PER_GROUP // 2) + pair
            cs = slice(g * SSD_GROUP_W + pair * LANES, g * SSD_GROUP_W + (pair + 1) * LANES)
            ps = slice(pair * LANES, (pair + 1) * LANES)
            x_pair = xs_b[:, cs]
            zero_x = jnp.zeros_like(x_pair)
            x_stack = jnp.concatenate([jnp.where(first_head, x_pair, zero_x),
                                       jnp.where(first_head, zero_x, x_pair)], axis=0)
            state = state_ref[d, g, :, ps]
            y = _dot(lhs_y[n], x_stack) + e_in[n] * _dot(c_g, state.astype(BF16))
            state_ref[d, g, :, ps] = state * e_in[n][end:end + 1, :] + _dot(lhs_s[n], x_stack)
            if d == 0:
                y = y + xs[:, cs] * dskip_ref[:, cs]
            y_ref[:, cs] = y


def _ssd_scan_kernel(xs_f, bc_f, dt_f, dtt_f, xs_b, bc_b, dt_b, dtt_b,
                     alog_row_ref, alog_col_ref, dskip_ref, yf_ref, yb_ref, state_ref):
    @pl.when(pl.program_id(0) == 0)
    def _():
        state_ref[...] = jnp.zeros_like(state_ref)

    params = (alog_row_ref, alog_col_ref, dskip_ref, state_ref)
    _ssd_direction(0, xs_f, bc_f, dt_f, dtt_f, *params, yf_ref)
    _ssd_direction(1, xs_b, bc_b, dt_b, dtt_b, *params, yb_ref)


def _ssd_scan(xs, bc, dt, dtt, alog_row, alog_col, dskip):
    t = SSD_CHUNK

    def chunk_specs(order):
        return [
            pl.BlockSpec((t, SSD_DI), lambda i: (order(i), 0)),
            pl.BlockSpec((t, 2 * SSD_GN), lambda i: (order(i), 0)),
            pl.BlockSpec((t, SSD_DT_PAD), lambda i: (order(i), 0)),
            pl.BlockSpec((SSD_DT_PAD, t), lambda i: (0, order(i))),
        ]

    y_shape = jax.ShapeDtypeStruct((ROWS, SSD_DI), F32)
    return pl.pallas_call(
        _ssd_scan_kernel,
        name="ssd_scan",
        grid=(_N_CHUNKS,),
        in_specs=chunk_specs(_fwd_chunk) + chunk_specs(_bwd_chunk) + [
            _const_spec((1, SSD_DT_PAD)),
            _const_spec((SSD_DT_PAD, 1)),
            _const_spec((1, SSD_DI)),
        ],
        out_specs=[
            pl.BlockSpec((t, SSD_DI), lambda i: (_fwd_chunk(i), 0)),
            pl.BlockSpec((t, SSD_DI), lambda i: (_bwd_chunk(i), 0)),
        ],
        out_shape=[y_shape, y_shape],
        scratch_shapes=[pltpu.VMEM((2, SSD_GROUPS, SSD_STATE, SSD_GROUP_W), F32)],
        compiler_params=pltpu.CompilerParams(
            dimension_semantics=("arbitrary",), vmem_limit_bytes=VMEM_LIMIT),
    )(xs, bc, dt, dtt, xs, bc, dt, dtt, alog_row, alog_col, dskip)


def _ssd_out_kernel(yf_ref, yb_ref, wz_ref, ng_ref, w_ref, pre_g_ref, pg_ref, mod_ref, *refs, split):
    r, (out_ref,) = _residual_tile(split, refs)
    h = _prenorm(r, pre_g_ref, mod_ref)
    parts = []
    for g in range(SSD_GROUPS):
        gs = slice(g * SSD_GROUP_W, (g + 1) * SSD_GROUP_W)
        yg = (yf_ref[:, gs] + yb_ref[:, gs]) * _silu(_dot(h, wz_ref[0, :, gs]))
        parts.append(yg * lax.rsqrt(jnp.mean(yg * yg, axis=-1, keepdims=True) + RMS_EPS))
    yn = (jnp.concatenate(parts, axis=1) * ng_ref[...]).astype(BF16)
    _postnorm_residual(_dot(yn, w_ref[0]), r, pg_ref, mod_ref, out_ref)


def _ssd_out(yf, yb, w_in, norm_g, w_out, w_layer, resid, pre_g, post_g, mod, layer):
    split = isinstance(resid, tuple)
    return pl.pallas_call(
        functools.partial(_ssd_out_kernel, split=split),
        name="ssd_out",
        grid=(ROWS // ROW_TILE,),
        in_specs=[
            _row_spec(SSD_DI), _row_spec(SSD_DI),
            _layer_spec((D_MODEL, SSD_DI), w_layer),
            _const_spec((1, SSD_DI)),
            _layer_spec((SSD_DI, D_MODEL), w_layer),
            _const_spec((1, D_MODEL)),
            _const_spec((1, D_MODEL)),
            _layer_spec((8, 3 * D_MODEL), layer),
        ] + _residual_specs(split),
        out_specs=_row_spec(D_MODEL),
        out_shape=jax.ShapeDtypeStruct((ROWS, D_MODEL), F32),
        compiler_params=pltpu.CompilerParams(vmem_limit_bytes=VMEM_LIMIT),
    )(yf, yb, w_in, norm_g, w_out, pre_g, post_g, mod, *(resid if split else (resid,)))


def _na_in_kernel(x_ref, g_ref, mod_ref, w_ref, qkv_ref, gate_ref):
    h = _prenorm(x_ref[...], g_ref, mod_ref)
    scale = NA_HEADDIM ** -0.5 * LOG2_E
    qkv_ref[:, :NA_DI] = (_dot(h, w_ref[0, :, :NA_DI]) * scale).astype(BF16)
    for c in range(NA_DI, 3 * NA_DI, _COL_CHUNK):
        qkv_ref[:, c:c + _COL_CHUNK] = _dot(h, w_ref[0, :, c:c + _COL_CHUNK]).astype(BF16)
    gate_ref[...] = _dot(h, w_ref[0, :, 3 * NA_DI:])


def _na_in(r, pre_g, mod, layer, w_in, w_layer):
    return pl.pallas_call(
        _na_in_kernel,
        name="na_in",
        grid=(ROWS // ROW_TILE,),
        in_specs=[
            _row_spec(D_MODEL),
            _const_spec((1, D_MODEL)),
            _layer_spec((8, 3 * D_MODEL), layer),
            _layer_spec((D_MODEL, 4 * NA_DI), w_layer, single_buffer=True),
        ],
        out_specs=[_row_spec(3 * NA_DI), _row_spec(NA_DI)],
        out_shape=[
            jax.ShapeDtypeStruct((ROWS, 3 * NA_DI), BF16),
            jax.ShapeDtypeStruct((ROWS, NA_DI), F32),
        ],
        compiler_params=pltpu.CompilerParams(vmem_limit_bytes=VMEM_LIMIT),
    )(r, pre_g, mod, w_in)


_NA_BLOCKS = ROWS // NA_QBLK
_NA_WIN_ROWS = 3 * NA_QROWS
_NA_WIN_KEYS = WIN_R * GRID_W
_NA_BIAS_VARIANTS = WIN_R
_NA_STEP_W = NA_PAIRS_PER_STEP * LANES


def _na_bias_table(rpb):
    n = GRID_W
    cols = np.arange(n)
    col_start = np.clip(cols - WIN_C // 2, 0, n - WIN_C)
    cvalid = (cols[None, :] >= col_start[:, None]) & (cols[None, :] < col_start[:, None] + WIN_C)

    rpb = rpb.astype(F32)
    edge = n - WIN_C
    ext = jnp.concatenate([jnp.repeat(rpb[..., :1], edge, axis=-1), rpb,
                           jnp.repeat(rpb[..., -1:], edge + 1, axis=-1)], axis=-1)
    skew = jnp.tile(ext, (1, 1, n))[..., :n * (2 * n - 1)].reshape(NA_HEADS, 2 * WIN_R - 1, n, 2 * n - 1)
    toep = jnp.where(cvalid, skew[..., n - 1:] * LOG2_E, NEG_BIG)
    rows = toep.transpose(0, 2, 1, 3).reshape(NA_HEADS, n, (2 * WIN_R - 1) * n)
    table = jnp.stack([rows[:, :, v * n:v * n + _NA_WIN_KEYS] for v in range(_NA_BIAS_VARIANTS)],
                      axis=1)
    return table.reshape(NA_HEADS // 2, 2, _NA_BIAS_VARIANTS, n, _NA_WIN_KEYS).transpose(
        0, 2, 1, 3, 4).reshape(NA_HEADS // 2, _NA_BIAS_VARIANTS, 2 * n, _NA_WIN_KEYS)


def _softmax_stats(scores):
    m = functools.reduce(jnp.maximum, [jnp.max(x, axis=-1, keepdims=True) for x in scores])
    p = [jnp.exp2(x - m) for x in scores]
    inv = 1.0 / functools.reduce(jnp.add, [jnp.sum(x, axis=-1, keepdims=True) for x in p])
    return [x.astype(BF16) for x in p], inv


def _na_kernel(cur_ref, prev_ref, next_ref, ctx_ref, gate_ref, bias_ref, o_ref, kwin_ref, vwin_ref):
    rb = pl.program_id(1)
    k_cols, v_cols = slice(NA_DI, 2 * NA_DI), slice(2 * NA_DI, 3 * NA_DI)

    def keys(ref, ls):
        return ref[:, NA_DI + ls.start:NA_DI + ls.stop]

    def vals(ref, ls):
        return ref[:, 2 * NA_DI + ls.start:2 * NA_DI + ls.stop]

    lane = lax.broadcasted_iota(jnp.int32, (1, LANES), 1)
    first_head = lane < NA_HEADDIM
    lss = [slice(pair * LANES, (pair + 1) * LANES) for pair in range(NA_PAIRS_PER_STEP)]
    n, n2 = GRID_W, 2 * GRID_W

    def stacked_queries(pair):
        q = cur_ref[:, lss[pair]]
        zero = jnp.zeros_like(q)
        q0 = jnp.where(first_head, q, zero)
        q1 = jnp.where(first_head, zero, q)
        parts = []
        for qr in range(NA_QROWS):
            parts += [q0[qr * n:(qr + 1) * n], q1[qr * n:(qr + 1) * n]]
        return jnp.concatenate(parts, axis=0)

    def finish(pair, o):
        ls = lss[pair]
        for qr in range(NA_QROWS):
            rows = slice(qr * n, (qr + 1) * n)
            merged = jnp.where(first_head, o[qr * n2:qr * n2 + n], o[qr * n2 + n:(qr + 1) * n2])
            o_ref[rows, ls] = (merged * _silu(gate_ref[rows, ls])).astype(BF16)

    @pl.when(rb == 0)
    def _():
        for pair in range(NA_PAIRS_PER_STEP):
            p, inv = _softmax_stats([_dot_nt(stacked_queries(pair), keys(ctx_ref, lss[pair]))])
            finish(pair, _dot(p[0], vals(ctx_ref, lss[pair])) * inv)

    @pl.when(rb > 0)
    def _():
        for j, ref in enumerate((prev_ref, cur_ref, next_ref)):
            kwin_ref[j * NA_QBLK:(j + 1) * NA_QBLK, :] = ref[:, k_cols]
            vwin_ref[j * NA_QBLK:(j + 1) * NA_QBLK, :] = ref[:, v_cols]
        r0 = (rb - 1) * NA_QROWS
        wins, variants = [], []
        for qr in range(NA_QROWS):
            r = r0 + qr
            start = jnp.clip(r - WIN_R // 2, 0, GRID_H - WIN_R)
            variants.append(start - r + (WIN_R - 1))
            wins.append(pl.ds(pl.multiple_of((start - (r0 - NA_QROWS)) * GRID_W, GRID_W), _NA_WIN_KEYS))
        srows = [slice(qr * n2, (qr + 1) * n2) for qr in range(NA_QROWS)]
        scores = []
        for pair in range(NA_PAIRS_PER_STEP):
            q = stacked_queries(pair)
            s_win = jnp.concatenate(
                [_dot_nt(q[srows[qr]], kwin_ref[wins[qr], lss[pair]]) + bias_ref[pair, variants[qr]]
                 for qr in range(NA_QROWS)], axis=0)
            scores.append([s_win, _dot_nt(q, keys(ctx_ref, lss[pair]))])
        stats = [_softmax_stats(sc) for sc in scores]
        for pair, (p, inv) in enumerate(stats):
            o_win = jnp.concatenate(
                [_dot(p[0][srows[qr]], vwin_ref[wins[qr], lss[pair]]) for qr in range(NA_QROWS)], axis=0)
            finish(pair, (o_win + _dot(p[1], vals(ctx_ref, lss[pair]))) * inv)


def _na_attention(qkv, gate, bias):
    assert _NA_STEP_W == NA_DI

    def slab(row_fn):
        return pl.BlockSpec((NA_QBLK, 3 * NA_DI), lambda hq, rb: (row_fn(rb), 0))

    return pl.pallas_call(
        _na_kernel,
        name="na_attention",
        grid=(1, _NA_BLOCKS),
        in_specs=[
            slab(lambda rb: rb),
            slab(lambda rb: jnp.maximum(rb - 1, 0)),
            slab(lambda rb: jnp.minimum(rb + 1, _NA_BLOCKS - 1)),
            slab(lambda rb: 0),
            pl.BlockSpec((NA_QBLK, NA_DI), lambda hq, rb: (rb, 0)),
            pl.BlockSpec((NA_PAIRS_PER_STEP, _NA_BIAS_VARIANTS, 2 * GRID_W, _NA_WIN_KEYS),
                         lambda hq, rb: (0, 0, 0, 0), pipeline_mode=pl.Buffered(1)),
        ],
        out_specs=pl.BlockSpec((NA_QBLK, NA_DI), lambda hq, rb: (rb, 0)),
        out_shape=jax.ShapeDtypeStruct((ROWS, NA_DI), BF16),
        scratch_shapes=[pltpu.VMEM((_NA_WIN_ROWS * GRID_W, NA_DI), BF16),
                        pltpu.VMEM((_NA_WIN_ROWS * GRID_W, NA_DI), BF16)],
        compiler_params=pltpu.CompilerParams(vmem_limit_bytes=VMEM_LIMIT),
    )(qkv, qkv, qkv, qkv, gate, bias)


def _na_out_kernel(a_ref, w_ref, r_ref, pg_ref, mod_ref, out_ref):
    _postnorm_residual(_dot(a_ref[...], w_ref[0]), r_ref[...], pg_ref, mod_ref, out_ref)


def _na_out(a, w_out, w_layer, r, post_g, mod, layer, latent_only):
    if latent_only:
        n_ctx = CTX_LEN // ROW_TILE
        out_spec = pl.BlockSpec((ROW_TILE, D_MODEL), lambda i: (jnp.maximum(i - n_ctx, 0), 0))
        out_rows = SEQ
    else:
        out_spec, out_rows = _row_spec(D_MODEL), ROWS
    return pl.pallas_call(
        _na_out_kernel,
        name="na_out",
        grid=(ROWS // ROW_TILE,),
        in_specs=[
            _row_spec(NA_DI),
            _layer_spec((NA_DI, D_MODEL), w_layer),
            _row_spec(D_MODEL),
            _const_spec((1, D_MODEL)),
            _layer_spec((8, 3 * D_MODEL), layer),
        ],
        out_specs=out_spec,
        out_shape=jax.ShapeDtypeStruct((out_rows, D_MODEL), F32),
        compiler_params=pltpu.CompilerParams(
            dimension_semantics=("arbitrary",), vmem_limit_bytes=VMEM_LIMIT),
    )(a, w_out, r, post_g, mod)


def _pad_lanes_row(v):
    flat = v.astype(F32).reshape(1, 2 * SSD_HEADS)
    return jnp.pad(flat, ((0, 0), (0, SSD_DT_PAD - 2 * SSD_HEADS)))


def kernel(x, c, ctx, c_ctx, ada_w, ada_b, pre_g, post_g, ssd_w_in, ssd_conv_w, ssd_conv_b,
           ssd_dt_bias, ssd_a_log, ssd_d, ssd_norm_g, ssd_w_out, na_w_in, na_rpb, na_w_out):
    assert x.shape == (1, SEQ, D_MODEL) and ctx.shape == (1, CTX_LEN, D_MODEL)
    r = (ctx[0], x[0])
    cvec = jnp.zeros((8, D_MODEL), F32).at[0].set(c[0]).at[1].set(c_ctx)
    mod = _modulation(cvec, ada_w, ada_b)
    ssd_w_in_b, ssd_w_out_b = ssd_w_in.astype(BF16), ssd_w_out.astype(BF16)
    na_w_in_b, na_w_out_b = na_w_in.astype(BF16), na_w_out.astype(BF16)
    dt_pad = SSD_DT_PAD - 2 * SSD_HEADS

    for i in range(DEPTH):
        j = i // 2
        pg = pre_g[i].reshape(1, D_MODEL)
        qg = post_g[i].reshape(1, D_MODEL)
        if i % 2 == 0:
            w_dt_t = jnp.pad(ssd_w_in_b[j, :, SSD_DI + SSD_CONV_CH:].T, ((0, dt_pad), (0, 0)))
            dtb_row = _pad_lanes_row(ssd_dt_bias[j])
            alog_row = _pad_lanes_row(ssd_a_log[j])
            xs, bc, dt, dtt = _ssd_in(
                r, pg, mod, i, ssd_w_in_b, j, w_dt_t, ssd_conv_w[j],
                ssd_conv_b[j].reshape(1, SSD_CONV_CH), dtb_row, dtb_row.reshape(SSD_DT_PAD, 1))
            yf, yb = _ssd_scan(
                xs, bc, dt, dtt, alog_row, alog_row.reshape(SSD_DT_PAD, 1),
                jnp.repeat(ssd_d[j].astype(F32), SSD_HEADDIM).reshape(1, SSD_DI))
            r = _ssd_out(yf, yb, ssd_w_in_b, ssd_norm_g[j].reshape(1, SSD_DI), ssd_w_out_b, j,
                         r, pg, qg, mod, i)
        else:
            qkv, gate = _na_in(r, pg, mod, i, na_w_in_b, j)
            a = _na_attention(qkv, gate, _na_bias_table(na_rpb[j]))
            r = _na_out(a, na_w_out_b, j, r, qg, mod, i, latent_only=(i == DEPTH - 1))
    return r[None]
```
